```python
import math
import jax, jax.numpy as jnp
from jax import lax
import numpy as np

D_MODEL = 2048
BATCH = 2
SEQ = 16384
DEPTH = 1
DEC_BATCH = 32
DEC_SEQ = 64
PAST_LEN = 2048

CHUNK = 64
Q_BLOCK = 128
N_HEADS_A = 8
HEAD_DIM_A = 64
VAL_DIM_A = 2 * HEAD_DIM_A
WIDTH_A = N_HEADS_A * VAL_DIM_A
ROT_DIM = HEAD_DIM_A // 4
ROPE_THETA = 500000.0
HEAD_DIM_B = 64
WIDTH_B = D_MODEL - WIDTH_A
N_HEADS_B = WIDTH_B // HEAD_DIM_B
LORA_W = 64
LORA_A = 64
LORA_G = 64
ATTN_COLS = 3 * WIDTH_A
RWKV_COLS = 3 * WIDTH_B + LORA_W + LORA_A + LORA_G
IN_COLS = ATTN_COLS + RWKV_COLS
D_FF = -(-8 * D_MODEL // (3 * 256)) * 256
RMS_EPS = 1e-6
GN_EPS = 64e-5

kernel_name = "hybrid_diffattn_rwkv7_stream_step"


def rmsnorm(x, g):
    xf = x.astype(jnp.float32)
    y = xf * lax.rsqrt(jnp.mean(xf * xf, axis=-1, keepdims=True) + RMS_EPS)
    return (y * g.astype(jnp.float32)).astype(x.dtype)


def rope_partial(x, pos):
    half = ROT_DIM // 2
    inv = jnp.float32(ROPE_THETA) ** (-jnp.arange(0, ROT_DIM, 2, dtype=jnp.float32) / ROT_DIM)
    ang = pos.astype(jnp.float32)[:, None] * inv
    cos = jnp.cos(ang)[:, None, None, :]
    sin = jnp.sin(ang)[:, None, None, :]
    xr = x[..., :ROT_DIM].astype(jnp.float32)
    x1, x2 = xr[..., :half], xr[..., half:]
    rot = jnp.concatenate([x1 * cos - x2 * sin, x2 * cos + x1 * sin], axis=-1)
    return jnp.concatenate([rot.astype(x.dtype), x[..., ROT_DIM:]], axis=-1)


def diff_attend(q, k, v, q_chunk, k_chunk, lam):
    s = jnp.einsum('bqhcd,bkhcd->bchqk', q, k).astype(jnp.float32) * (HEAD_DIM_A ** -0.5)
    mask = k_chunk[None, :] <= q_chunk[:, None]
    s = jnp.where(mask, s, -jnp.inf)
    p = jax.nn.softmax(s, axis=-1)
    a = p[:, 0] - lam * p[:, 1]
    return jnp.einsum('bhqk,bkhe->bqhe', a.astype(v.dtype), v)


def rwkv_scan(S0, r, w, k, v, a, b):
    def step(S, inp):
        r_t, w_t, k_t, v_t, a_t, b_t = inp
        sa = jnp.einsum('bhvk,bhk->bhv', S, a_t)
        S = S * w_t[:, :, None, :] + sa[..., None] * b_t[:, :, None, :] + v_t[..., None] * k_t[:, :, None, :]
        return S, jnp.einsum('bhvk,bhk->bhv', S, r_t)
    xs = tuple(jnp.moveaxis(t.astype(jnp.float32), 1, 0) for t in (r, w, k, v, a, b))
    S, y = lax.scan(step, S0.astype(jnp.float32), xs)
    return S, jnp.moveaxis(y, 0, 1)


def rwkv_mix(p, shift0, S0, mu, w0, w2, a0, a2, g2, k_k, k_a, r_k, lnx_g, lnx_b):
    B, T, _ = p.shape
    prev = jnp.concatenate([shift0.astype(p.dtype), p[:, :-1]], axis=1)
    xs = p + (prev - p) * mu
    r, k, v, wd, ad, gd = jnp.split(
        xs, [WIDTH_B, 2 * WIDTH_B, 3 * WIDTH_B, 3 * WIDTH_B + LORA_W, 3 * WIDTH_B + LORA_W + LORA_A], axis=-1)
    w = -jax.nn.softplus(-(w0 + jnp.tanh(wd) @ w2)) - 0.5
    decay = jnp.exp(-jnp.exp(w.astype(jnp.float32)))
    a = jax.nn.sigmoid(a0 + ad @ a2)
    g = jax.nn.sigmoid(gd) @ g2
    kk = (k * k_k).astype(jnp.float32).reshape(B, T, N_HEADS_B, HEAD_DIM_B)
    kk = kk / jnp.maximum(jnp.sqrt(jnp.sum(kk * kk, axis=-1, keepdims=True)), 1e-12)
    k = k * (1 + (a - 1) * k_a)
    hd = lambda t: t.reshape(B, T, N_HEADS_B, HEAD_DIM_B)
    r_h, k_h, v_h, a_h, d_h = hd(r), hd(k), hd(v), hd(a), hd(decay)
    S, y = rwkv_scan(S0, r_h, d_h, k_h, v_h, -kk, kk * a_h.astype(jnp.float32))
    mean = jnp.mean(y, axis=-1, keepdims=True)
    var = jnp.mean(jnp.square(y - mean), axis=-1, keepdims=True)
    yn = ((y - mean) * lax.rsqrt(var + GN_EPS)).reshape(B, T, WIDTH_B) * lnx_g + lnx_b
    bonus = jnp.sum((r_h * k_h * r_k).astype(jnp.float32), axis=-1, keepdims=True) * v_h.astype(jnp.float32)
    out = (yn + bonus.reshape(B, T, WIDTH_B)) * g.astype(jnp.float32)
    return out.astype(p.dtype), S, p[:, -1:]


def layer(x, pos, past_k, past_v, past_pos, S0, shift0, lam_init,
          norm1_g, w_in, q_norm_g, k_norm_g, lambda_q1, lambda_k1, lambda_q2, lambda_k2, subln_g,
          mu_rwkv, w0, w2, a0, a2, g2, k_k, k_a, r_k, lnx_g, lnx_b,
          w_out, norm2_g, w_gate, w_up, w_down):
    B, T, _ = x.shape
    xn = rmsnorm(x, norm1_g)
    P = xn @ w_in
    p_attn, p_rwkv = P[..., :ATTN_COLS], P[..., ATTN_COLS:]
    q, k, v = jnp.split(p_attn, 3, axis=-1)
    q = rope_partial(rmsnorm(q.reshape(B, T, N_HEADS_A, 2, HEAD_DIM_A), q_norm_g), pos)
    k = rope_partial(rmsnorm(k.reshape(B, T, N_HEADS_A, 2, HEAD_DIM_A), k_norm_g), pos)
    k_new = k.reshape(B, T, N_HEADS_A, 2 * HEAD_DIM_A)
    v_new = v.reshape(B, T, N_HEADS_A, VAL_DIM_A)
    k_all = jnp.concatenate([past_k.astype(x.dtype), k_new], axis=1)
    k_all = k_all.reshape(B, k_all.shape[1], N_HEADS_A, 2, HEAD_DIM_A)
    v_all = jnp.concatenate([past_v.astype(x.dtype), v_new], axis=1)
    k_chunk = jnp.concatenate([past_pos, pos]) // CHUNK
    q_chunk = pos // CHUNK
    lam = (jnp.exp(jnp.sum(lambda_q1.astype(jnp.float32) * lambda_k1.astype(jnp.float32)))
           - jnp.exp(jnp.sum(lambda_q2.astype(jnp.float32) * lambda_k2.astype(jnp.float32))) + lam_init)
    if T > Q_BLOCK and T % Q_BLOCK == 0:
        nb = T // Q_BLOCK
        qb = q.reshape(B, nb, Q_BLOCK, N_HEADS_A, 2, HEAD_DIM_A).swapaxes(0, 1)
        qcb = q_chunk.reshape(nb, Q_BLOCK)
        o = lax.map(lambda args: diff_attend(args[0], k_all, v_all, args[1], k_chunk, lam), (qb, qcb))
        o = o.swapaxes(0, 1).reshape(B, T, N_HEADS_A, VAL_DIM_A)
    else:
        o = diff_attend(q, k_all, v_all, q_chunk, k_chunk, lam)
    attn_out = (rmsnorm(o, subln_g) * (1 - lam_init)).reshape(B, T, WIDTH_A)
    rwkv_out, S, shift = rwkv_mix(p_rwkv, shift0, S0, mu_rwkv, w0, w2, a0, a2, g2,
                                  k_k, k_a, r_k, lnx_g, lnx_b)
    h = x + jnp.concatenate([attn_out, rwkv_out], axis=-1) @ w_out
    hn = rmsnorm(h, norm2_g)
    y = h + (jax.nn.silu(hn @ w_gate) * (hn @ w_up)) @ w_down
    return y, k_new, v_new, S.astype(x.dtype), shift


def setup_inputs(seed: int = 0) -> dict:
    key = jax.random.key(seed)
    ks = jax.random.split(key, 40)
    n = lambda i, shape, s=1.0: jax.random.normal(ks[i], shape, jnp.float32) * s
    L = DEPTH
    return {
        "x_prompt": n(0, (BATCH, SEQ, D_MODEL)),
        "x_sample": n(1, (DEC_BATCH, DEC_SEQ, D_MODEL)),
        "cache_attn_k": n(2, (L, DEC_BATCH, PAST_LEN, N_HEADS_A, 2 * HEAD_DIM_A)),
        "cache_attn_v": n(3, (L, DEC_BATCH, PAST_LEN, N_HEADS_A, VAL_DIM_A)),
        "state_rwkv": n(4, (L, DEC_BATCH, N_HEADS_B, HEAD_DIM_B, HEAD_DIM_B), 0.3),
        "state_rwkv_shift": n(5, (L, DEC_BATCH, 1, RWKV_COLS)),
        "norm1_g": 1.0 + n(6, (L, D_MODEL), 0.05),
        "w_in": n(7, (L, D_MODEL, IN_COLS), D_MODEL ** -0.5),
        "q_norm_g": 1.0 + n(8, (L, 2, HEAD_DIM_A), 0.05),
        "k_norm_g": 1.0 + n(9, (L, 2, HEAD_DIM_A), 0.05),
        "lambda_q1": n(10, (L, HEAD_DIM_A), 0.1),
        "lambda_k1": n(11, (L, HEAD_DIM_A), 0.1),
        "lambda_q2": n(12, (L, HEAD_DIM_A), 0.1),
        "lambda_k2": n(13, (L, HEAD_DIM_A), 0.1),
        "subln_g": 1.0 + n(14, (L, VAL_DIM_A), 0.05),
        "mu_rwkv": jax.random.uniform(ks[15], (L, RWKV_COLS), jnp.float32),
        "w0": jax.random.uniform(ks[16], (L, WIDTH_B), jnp.float32, minval=-6.0, maxval=-1.0),
        "w2": n(17, (L, LORA_W, WIDTH_B), 0.5 * LORA_W ** -0.5),
        "a0": n(18, (L, WIDTH_B), 0.1),
        "a2": n(19, (L, LORA_A, WIDTH_B), 0.5 * LORA_A ** -0.5),
        "g2": n(20, (L, LORA_G, WIDTH_B), LORA_G ** -0.5),
        "k_k": 0.85 + n(21, (L, WIDTH_B), 0.05),
        "k_a": 1.0 + n(22, (L, WIDTH_B), 0.05),
        "r_k": n(23, (L, N_HEADS_B, HEAD_DIM_B), 0.1),
        "lnx_g": 1.0 + n(24, (L, WIDTH_B), 0.05),
        "lnx_b": n(25, (L, WIDTH_B), 0.01),
        "w_out": n(26, (L, D_MODEL, D_MODEL), D_MODEL ** -0.5),
        "norm2_g": 1.0 + n(27, (L, D_MODEL), 0.05),
        "w_gate": n(28, (L, D_MODEL, D_FF), D_MODEL ** -0.5),
        "w_up": n(29, (L, D_MODEL, D_FF), D_MODEL ** -0.5),
        "w_down": n(30, (L, D_FF, D_MODEL), D_FF ** -0.5),
    }


def reference(x_prompt, x_sample, cache_attn_k, cache_attn_v, state_rwkv, state_rwkv_shift,
              norm1_g, w_in, q_norm_g, k_norm_g, lambda_q1, lambda_k1, lambda_q2, lambda_k2, subln_g,
              mu_rwkv, w0, w2, a0, a2, g2, k_k, k_a, r_k, lnx_g, lnx_b,
              w_out, norm2_g, w_gate, w_up, w_down):
    Bp, Tp, _ = x_prompt.shape
    Bs, Ts, _ = x_sample.shape
    past_len = cache_attn_k.shape[2]
    dt = x_prompt.dtype
    pos_p = jnp.arange(Tp, dtype=jnp.int32)
    pos_s = past_len + jnp.arange(Ts, dtype=jnp.int32)
    past_pos_s = jnp.arange(past_len, dtype=jnp.int32)
    empty_k = jnp.zeros((Bp, 0, N_HEADS_A, 2 * HEAD_DIM_A), dt)
    empty_v = jnp.zeros((Bp, 0, N_HEADS_A, VAL_DIM_A), dt)
    empty_pos = jnp.zeros((0,), jnp.int32)
    S0_p = jnp.zeros((Bp, N_HEADS_B, HEAD_DIM_B, HEAD_DIM_B), jnp.float32)
    shift0_p = jnp.zeros((Bp, 1, RWKV_COLS), dt)
    xp, xs = x_prompt, x_sample
    kp_l, vp_l, Sp_l, shp_l, ks_l, vs_l, Ss_l, shs_l = [], [], [], [], [], [], [], []
    for l in range(DEPTH):
        lam_init = 0.8 - 0.6 * math.exp(-0.3 * l)
        lw = (norm1_g[l], w_in[l], q_norm_g[l], k_norm_g[l], lambda_q1[l], lambda_k1[l],
              lambda_q2[l], lambda_k2[l], subln_g[l], mu_rwkv[l], w0[l], w2[l], a0[l], a2[l], g2[l],
              k_k[l], k_a[l], r_k[l], lnx_g[l], lnx_b[l], w_out[l], norm2_g[l], w_gate[l], w_up[l], w_down[l])
        xp, kp, vp, Sp, shp = layer(xp, pos_p, empty_k, empty_v, empty_pos, S0_p, shift0_p, lam_init, *lw)
        xs, ks, vs, Ss, shs = layer(xs, pos_s, cache_attn_k[l], cache_attn_v[l], past_pos_s,
                                    state_rwkv[l], state_rwkv_shift[l], lam_init, *lw)
        kp_l.append(kp); vp_l.append(vp); Sp_l.append(Sp); shp_l.append(shp)
        ks_l.append(ks); vs_l.append(vs); Ss_l.append(Ss); shs_l.append(shs)
    return (xp, xs,
            jnp.stack(kp_l), jnp.stack(vp_l), jnp.stack(Sp_l), jnp.stack(shp_l),
            jnp.stack(ks_l), jnp.stack(vs_l), jnp.stack(Ss_l), jnp.stack(shs_l))
```

```python
import functools
import math

import jax
import jax.numpy as jnp
from jax import lax
from jax.experimental import pallas as pl
from jax.experimental.pallas import tpu as pltpu

F32 = jnp.float32
BF16 = jnp.bfloat16

LANES = 128
HEAD_A = 128
HALF_A = 64
ROT_HALF = 8
ROPE_THETA = 500000.0
HEAD_B = 64
CHUNK_MASK = 64
CHUNK_B = 64
LORA = 64
RMS_EPS = 1e-6
GN_EPS = 64e-5
NEG_BIG = -1e30
VMEM_LIMIT = 56 * 1024 * 1024

NN = ((1,), (0,))
NT = ((1,), (1,))


def _dg(a, b, dims=NN):
    return lax.dot_general(a, b, (dims, ((), ())), preferred_element_type=F32)


def _split(a):
    hi = a.astype(BF16)
    lo = (a - hi.astype(F32)).astype(BF16)
    return hi, lo


def _mm3(a, b, dims=NN):
    ah, al = _split(a)
    bh, bl = _split(b)
    return _dg(ah, bh, dims) + (_dg(ah, bl, dims) + _dg(al, bh, dims))


def _mm2(a, b_exact):
    ah, al = _split(a)
    return _dg(ah, b_exact) + _dg(al, b_exact)


def _group_matrix(n, group, value):
    r = lax.broadcasted_iota(jnp.int32, (n, n), 0) // group
    c = lax.broadcasted_iota(jnp.int32, (n, n), 1) // group
    return jnp.where(r == c, value, 0.0).astype(BF16)


def _sigmoid(x):
    return 1.0 / (1.0 + jnp.exp(-x))


def _params(*sem):
    return pltpu.CompilerParams(dimension_semantics=sem, vmem_limit_bytes=VMEM_LIMIT)


def _inproj_kernel(*refs, mode, scale):
    if mode == "rope":
        x_ref, g_ref, w_ref, hg_ref, cos_ref, sa_ref, sb_ref = refs[:7]
        outs = refs[7:-1]
    else:
        x_ref, g_ref, w_ref = refs[:3]
        outs = refs[3:-1]
    xn_ref = refs[-1]

    @pl.when(pl.program_id(1) == 0)
    def _():
        x = x_ref[...]
        ms = jnp.mean(x * x, axis=-1, keepdims=True)
        xn_ref[...] = (x * lax.rsqrt(ms + RMS_EPS) * g_ref[...]).astype(BF16)

    acc = _dg(xn_ref[...], w_ref[...])
    if mode == "rope":
        seg = _group_matrix(LANES, HALF_A, 1.0 / HALF_A)
        hg = hg_ref[...] * scale
        cos, sa, sb = cos_ref[...], sa_ref[...], sb_ref[...]
        for g in range(acc.shape[1] // LANES):
            xg = acc[:, g * LANES:(g + 1) * LANES]
            ms = _mm2(xg * xg, seg)
            xg = xg * lax.rsqrt(ms + RMS_EPS) * hg
            up = pltpu.roll(xg, LANES - ROT_HALF, axis=1)
            dn = pltpu.roll(xg, ROT_HALF, axis=1)
            xg = xg * cos + up * sa + dn * sb
            for o in outs:
                o[:, g * LANES:(g + 1) * LANES] = xg.astype(o.dtype)
    else:
        for o in outs:
            o[...] = acc.astype(o.dtype)


def _inproj(x2d, g, w, out_dtypes, *, tm, tn, rope=None, scale=1.0):
    n, d = x2d.shape
    ncol = w.shape[1]
    grid = (n // tm, ncol // tn)
    in_specs = [
        pl.BlockSpec((tm, d), lambda i, j: (i, 0)),
        pl.BlockSpec((1, d), lambda i, j: (0, 0)),
        pl.BlockSpec((d, tn), lambda i, j: (0, j)),
    ]
    args = [x2d, g.reshape(1, d), w]
    if rope is not None:
        hg, cos, sa, sb = rope
        nblk = cos.shape[0] // tm
        in_specs.append(pl.BlockSpec((1, LANES), lambda i, j: (0, 0)))
        args.append(hg)
        for t in (cos, sa, sb):
            in_specs.append(pl.BlockSpec((tm, LANES), lambda i, j: (i % nblk, 0)))
            args.append(t)
    out_shape = [jax.ShapeDtypeStruct((n, ncol), dt) for dt in out_dtypes]
    out_specs = [pl.BlockSpec((tm, tn), lambda i, j: (i, j)) for _ in out_dtypes]
    return pl.pallas_call(
        functools.partial(_inproj_kernel, mode="rope" if rope is not None else "plain", scale=scale),
        grid=grid, in_specs=in_specs, out_specs=out_specs, out_shape=out_shape,
        scratch_shapes=[pltpu.VMEM((tm, d), BF16)],
        compiler_params=_params("parallel", "arbitrary"),
        name="inproj_rope" if rope is not None else "inproj",
    )(*args)


def _rope_tables(pos, rows):
    inv = jnp.float32(ROPE_THETA) ** (-jnp.arange(0, 2 * ROT_HALF, 2, dtype=F32) / (2 * ROT_HALF))
    ang = pos.astype(F32)[:, None] * inv
    cos8, sin8 = jnp.cos(ang), jnp.sin(ang)
    t = pos.shape[0]
    one = jnp.ones((t, HALF_A - 2 * ROT_HALF), F32)
    zero8 = jnp.zeros((t, ROT_HALF), F32)
    zrest = jnp.zeros((t, HALF_A - 2 * ROT_HALF), F32)
    cos_h = jnp.concatenate([cos8, cos8, one], axis=1)
    sa_h = jnp.concatenate([-sin8, zero8, zrest], axis=1)
    sb_h = jnp.concatenate([zero8, sin8, zrest], axis=1)
    tabs = [jnp.concatenate([h, h], axis=1) for h in (cos_h, sa_h, sb_h)]
    if rows > t:
        tabs = [jnp.tile(h, (rows // t, 1)) for h in tabs]
    return tabs


def _stack_halves(q):
    lane = lax.broadcasted_iota(jnp.int32, q.shape, 1)
    zero = jnp.zeros_like(q)
    return jnp.concatenate([jnp.where(lane < HALF_A, q, zero), jnp.where(lane < HALF_A, zero, q)], axis=0)


def _lambda(lq1, lk1, lq2, lk2, lam_init):
    return (jnp.exp(jnp.sum(lq1 * lk1, axis=-1, keepdims=True))
            - jnp.exp(jnp.sum(lq2 * lk2, axis=-1, keepdims=True)) + lam_init)


def _attn_finish(acc, l, lam, g, lam_init, tq):
    o = acc[:tq] / l[:tq] - lam * (acc[tq:] / l[tq:])
    ms = jnp.mean(o * o, axis=-1, keepdims=True)
    return o * lax.rsqrt(ms + RMS_EPS) * (g * (1.0 - lam_init))


def _attn_prompt_kernel(q_ref, k_ref, v_ref, lq1, lk1, lq2, lk2, g_ref, o_ref, *, tq, lam_init):
    i = pl.program_id(2)
    qs = _stack_halves(q_ref[...])

    def tile(j, carry, masked):
        m, l, acc = carry
        start = pl.multiple_of(j * tq, tq)
        k = k_ref[pl.ds(start, tq), :]
        v = v_ref[pl.ds(start, tq), :]
        s = _dg(qs, k, NT)
        if masked:
            row = lax.broadcasted_iota(jnp.int32, s.shape, 0)
            col = lax.broadcasted_iota(jnp.int32, s.shape, 1)
            qc = jnp.where(row >= tq, row - tq, row) // CHUNK_MASK
            s = jnp.where(col // CHUNK_MASK <= qc, s, NEG_BIG)
        m_new = jnp.maximum(m, jnp.max(s, axis=-1, keepdims=True))
        p = jnp.exp(s - m_new)
        alpha = jnp.exp(m - m_new)
        l = alpha * l + jnp.sum(p, axis=-1, keepdims=True)
        acc = alpha * acc + _dg(p.astype(BF16), v)
        return m_new, l, acc

    init = (jnp.full((2 * tq, 1), NEG_BIG, F32), jnp.zeros((2 * tq, 1), F32),
            jnp.zeros((2 * tq, HEAD_A), F32))
    carry = lax.fori_loop(0, i, lambda j, c: tile(j, c, False), init)
    _, l, acc = tile(i, carry, True)
    lam = _lambda(lq1[...], lk1[...], lq2[...], lk2[...], lam_init)
    o_ref[...] = _attn_finish(acc, l, lam, g_ref[...], lam_init, tq).astype(o_ref.dtype)


def _attn_prompt(q, k, v, lams, g, lam_init, *, tq):
    b, t, width = q.shape
    nh = width // HEAD_A
    small = pl.BlockSpec((1, HALF_A), lambda bi, h, i: (0, 0))
    return pl.pallas_call(
        functools.partial(_attn_prompt_kernel, tq=tq, lam_init=lam_init),
        grid=(b, nh, t // tq),
        in_specs=[
            pl.BlockSpec((None, tq, HEAD_A), lambda bi, h, i: (bi, i, h)),
            pl.BlockSpec((None, t, HEAD_A), lambda bi, h, i: (bi, 0, h)),
            pl.BlockSpec((None, t, HEAD_A), lambda bi, h, i: (bi, 0, h)),
            small, small, small, small,
            pl.BlockSpec((1, HEAD_A), lambda bi, h, i: (0, 0)),
        ],
        out_specs=pl.BlockSpec((None, tq, HEAD_A), lambda bi, h, i: (bi, i, h)),
        out_shape=jax.ShapeDtypeStruct((b, t, width), BF16),
        compiler_params=_params("parallel", "parallel", "arbitrary"),
        name="attn_prompt",
    )(q, k, v, *lams, g)


def _attn_sample_kernel(q_ref, kp_ref, vp_ref, kn_ref, vn_ref, lq1, lk1, lq2, lk2, g_ref, o_ref,
                        *, tq, past, lam_init):
    qs = _stack_halves(q_ref[...])
    s_p = _dg(qs, kp_ref[...].astype(BF16), NT)
    s_n = _dg(qs, kn_ref[...], NT)
    row = lax.broadcasted_iota(jnp.int32, s_n.shape, 0)
    col = lax.broadcasted_iota(jnp.int32, s_n.shape, 1)
    qc = (past + jnp.where(row >= tq, row - tq, row)) // CHUNK_MASK
    s_n = jnp.where((past + col) // CHUNK_MASK <= qc, s_n, NEG_BIG)
    m =jnp.maximum(jnp.max(s_p, axis=-1, keepdims=True), jnp.max(s_n, axis=-1, keepdims=True))
    p_p = jnp.exp(s_p - m)
    p_n = jnp.exp(s_n - m)
    l = jnp.sum(p_p, axis=-1, keepdims=True) + jnp.sum(p_n, axis=-1, keepdims=True)
    acc = _dg(p_p.astype(BF16), vp_ref[...].astype(BF16)) + _dg(p_n.astype(BF16), vn_ref[...])
    lam = _lambda(lq1[...], lk1[...], lq2[...], lk2[...], lam_init)
    o_ref[...] = _attn_finish(acc, l, lam, g_ref[...], lam_init, tq).astype(o_ref.dtype)


def _attn_sample(q, k_past, v_past, k_new, v_new, lams, g, lam_init):
    b, t, width = q.shape
    past = k_past.shape[1]
    nh = width // HEAD_A
    small = pl.BlockSpec((1, HALF_A), lambda bi, h: (0, 0))
    new = pl.BlockSpec((None, t, HEAD_A), lambda bi, h: (bi, 0, h))
    old = pl.BlockSpec((None, past, HEAD_A), lambda bi, h: (bi, 0, h))
    return pl.pallas_call(
        functools.partial(_attn_sample_kernel, tq=t, past=past, lam_init=lam_init),
        grid=(b, nh),
        in_specs=[new, old, old, new, new, small, small, small, small,
                  pl.BlockSpec((1, HEAD_A), lambda bi, h: (0, 0))],
        out_specs=new,
        out_shape=jax.ShapeDtypeStruct((b, t, width), BF16),
        compiler_params=_params("parallel", "parallel"),
        name="attn_sample",
    )(q, k_past, v_past, k_new, v_new, *lams, g)


def _stack_heads(x):
    lane = lax.broadcasted_iota(jnp.int32, x.shape, 1)
    return jnp.concatenate([jnp.where(lane < HEAD_B, x, 0.0), jnp.where(lane < HEAD_B, 0.0, x)], axis=0)


def _rwkv_kernel(pr_ref, pk_ref, pv_ref, pl_ref, sr_ref, sk_ref, sv_ref, sl_ref,
                 mr_ref, mk_ref, mv_ref, ml_ref,
                 w0_ref, a0_ref, kk_ref, ka_ref, rk_ref, lg_ref, lb_ref,
                 w2_ref, a2_ref, g2_ref, s0_ref,
                 out_ref, sout_ref,
                 s_sc, cr_sc, ck_sc, cv_sc, cl_sc, *, tb):
    t = pl.program_id(2)

    @pl.when(t == 0)
    def _():
        s_sc[...] = s0_ref[...]
        cr_sc[...] = sr_ref[...]
        ck_sc[...] = sk_ref[...]
        cv_sc[...] = sv_ref[...]
        cl_sc[...] = sl_ref[...]

    def shifted(p_ref, carry_ref, mu_ref):
        p = p_ref[...]
        prev = pltpu.roll(p, 1, axis=0)
        row = lax.broadcasted_iota(jnp.int32, p.shape, 0)
        prev = jnp.where(row == 0, carry_ref[...], prev)
        carry_ref[...] = p[tb - 1:tb, :]
        return p + (prev - p) * mu_ref[...]

    xr = shifted(pr_ref, cr_sc, mr_ref)
    xk = shifted(pk_ref, ck_sc, mk_ref)
    xv = shifted(pv_ref, cv_sc, mv_ref)
    xl = shifted(pl_ref, cl_sc, ml_ref)

    seg = _group_matrix(LANES, HEAD_B, 1.0)
    z = w0_ref[...] + _mm3(jnp.tanh(xl), w2_ref[...])
    nz = -z
    softplus = jnp.maximum(nz, 0.0) + jnp.log(1.0 + jnp.exp(-jnp.abs(nz)))
    logw = -jnp.exp(-softplus - 0.5)
    ag = _sigmoid(a0_ref[...] + _mm3(xl, a2_ref[...]))
    gate = _mm3(_sigmoid(xl), g2_ref[...])
    kk = xk * kk_ref[...]
    kk = kk / jnp.maximum(jnp.sqrt(_mm2(kk * kk, seg)), 1e-12)
    km = xk * (1.0 + (ag - 1.0) * ka_ref[...])
    bonus = _mm2(xr * km * rk_ref[...], seg) * xv

    c = CHUNK_B
    c2 = 2 * c
    ri = lax.broadcasted_iota(jnp.int32, (c, c), 0)
    ci = lax.broadcasted_iota(jnp.int32, (c, c), 1)
    tri = jnp.where(ci <= ri, 1.0, 0.0).astype(BF16)
    rs = lax.broadcasted_iota(jnp.int32, (c2, c2), 0) % c
    cs = lax.broadcasted_iota(jnp.int32, (c2, c2), 1) % c
    strict = cs < rs
    incl = cs <= rs
    eye = jnp.where(lax.broadcasted_iota(jnp.int32, (c2, c2), 0)
                    == lax.broadcasted_iota(jnp.int32, (c2, c2), 1), 1.0, 0.0)
    lane = lax.broadcasted_iota(jnp.int32, (c2, c2), 1) // HEAD_B
    sub = lax.broadcasted_iota(jnp.int32, (c2, c2), 0) // HEAD_B
    same_head = lane == sub

    s_bd = s_sc[...]
    ys = []
    for n in range(tb // c):
        sl = slice(n * c, (n + 1) * c)
        lw = logw[sl]
        h1 = lw.astype(BF16)
        r1 = lw - h1.astype(F32)
        h2 = r1.astype(BF16)
        h3 = (r1 - h2.astype(F32)).astype(BF16)
        cum = _dg(tri, h1) + (_dg(tri, h2) + _dg(tri, h3))
        w_in = jnp.exp(cum)
        w_ex = jnp.exp(cum - lw)
        w_inv = jnp.exp(-cum)
        w_last = w_in[c - 1:c, :]
        a_s = _stack_heads(-kk[sl] * w_ex)
        r_s = _stack_heads(xr[sl] * w_in)
        b_s = _stack_heads(kk[sl] * ag[sl] * w_inv)
        k_s = _stack_heads(km[sl] * w_inv)
        v_s = _stack_heads(xv[sl])
        gm = _mm3(jnp.concatenate([a_s, r_s], axis=0), jnp.concatenate([b_s, k_s], axis=0), NT)
        n_ab = jnp.where(strict, gm[:c2, :c2], 0.0)
        n_ak = jnp.where(strict, gm[:c2, c2:], 0.0)
        n_rb = jnp.where(incl, gm[c2:, :c2], 0.0)
        n_rk = jnp.where(incl, gm[c2:, c2:], 0.0)
        tinv = eye + n_ab
        npow = n_ab
        for _ in range(int(math.log2(c)) - 1):
            npow = _mm3(npow, npow)
            tinv = tinv + _mm3(tinv, npow)
        ta = _mm3(tinv, a_s)
        u0 = _mm3(tinv, _mm3(n_ak, v_s))
        u = _mm3(ta, s_bd, NT) + u0
        y_s = _mm3(r_s, s_bd, NT) + _mm3(n_rb, u) + _mm3(n_rk, v_s)
        ys.append(y_s[:c] + y_s[c:])
        upd = _mm3(jnp.concatenate([u, v_s], axis=0).T, jnp.concatenate([b_s, k_s], axis=0))
        s_bd = jnp.where(same_head, (s_bd + upd) * w_last, 0.0)
    s_sc[...] = s_bd

    @pl.when(t == pl.num_programs(2) - 1)
    def _():
        sout_ref[...] = s_bd

    y = jnp.concatenate(ys, axis=0) if len(ys) > 1 else ys[0]
    segm = _group_matrix(LANES, HEAD_B, 1.0 / HEAD_B)
    mean = _mm2(y, segm)
    d = y - mean
    var = _mm2(d * d, segm)
    yn = d * lax.rsqrt(var + GN_EPS) * lg_ref[...] + lb_ref[...]
    out_ref[...] = ((yn + bonus) * gate).astype(out_ref.dtype)


def _rwkv(p, shift0, s0_bd, mu, vecs, loras, *, tb):
    b, t, _ = p.shape
    width = vecs[0].shape[-1]
    ng = width // LANES
    lcol = 3 * width // (2 * LANES)

    def col(off):
        return lambda bi, g, ti: (bi, ti, off + g)

    def row(off):
        return lambda bi, g, ti: (bi, 0, off + g)

    def par(off):
        return lambda bi, g, ti: (0, off + g)

    in_specs = (
        [pl.BlockSpec((None, tb, LANES), col(k * ng)) for k in range(3)]
        + [pl.BlockSpec((None, tb, 2 * LANES), lambda bi, g, ti: (bi, ti, lcol))]
        + [pl.BlockSpec((None, 1, LANES), row(k * ng)) for k in range(3)]
        + [pl.BlockSpec((None, 1, 2 * LANES), lambda bi, g, ti: (bi, 0, lcol))]
        + [pl.BlockSpec((1, LANES), par(k * ng)) for k in range(3)]
        + [pl.BlockSpec((1, 2 * LANES), lambda bi, g, ti: (0, lcol))]
        + [pl.BlockSpec((1, LANES), par(0)) for _ in vecs]
        + [pl.BlockSpec((2 * LANES, LANES), par(0)) for _ in loras]
        + [pl.BlockSpec((None, None, LANES, LANES), lambda bi, g, ti: (bi, g, 0, 0))]
    )
    return pl.pallas_call(
        functools.partial(_rwkv_kernel, tb=tb),
        grid=(b, ng, t // tb),
        in_specs=in_specs,
        out_specs=[pl.BlockSpec((None, tb, LANES), col(0)),
                   pl.BlockSpec((None, None, LANES, LANES), lambda bi, g, ti: (bi, g, 0, 0))],
        out_shape=[jax.ShapeDtypeStruct((b, t, width), BF16),
                   jax.ShapeDtypeStruct((b, ng, LANES, LANES), F32)],
        scratch_shapes=[pltpu.VMEM((LANES, LANES), F32), pltpu.VMEM((1, LANES), F32),
                        pltpu.VMEM((1, LANES), F32), pltpu.VMEM((1, LANES), F32),
                        pltpu.VMEM((1, 2 * LANES), F32)],
        compiler_params=_params("parallel", "parallel", "arbitrary"),
        name="rwkv",
    )(p, p, p, p, shift0, shift0, shift0, shift0, mu, mu, mu, mu, *vecs, *loras, s0_bd)


def _outproj_kernel(x_ref, a_ref, b_ref, wa_ref, wb_ref, o_ref):
    o_ref[...] = x_ref[...] + (_dg(a_ref[...], wa_ref[...]) + _dg(b_ref[...], wb_ref[...]))


def _outproj(x2d, a, b, w, *, tm):
    n, d = x2d.shape
    half = a.shape[1]
    return pl.pallas_call(
        _outproj_kernel,
        grid=(n // tm,),
        in_specs=[pl.BlockSpec((tm, d), lambda i: (i, 0)),
                  pl.BlockSpec((tm, half), lambda i: (i, 0)),
                  pl.BlockSpec((tm, half), lambda i: (i, 0)),
                  pl.BlockSpec((half, d), lambda i: (0, 0)),
                  pl.BlockSpec((half, d), lambda i: (1, 0))],
        out_specs=pl.BlockSpec((tm, d), lambda i: (i, 0)),
        out_shape=jax.ShapeDtypeStruct((n, d), F32),
        compiler_params=_params("parallel"),
        name="outproj",
    )(x2d, a, b, w, w)


def _ffn_kernel(h_ref, g_ref, wg_ref, wu_ref, wd_ref, o_ref, hn_ref):
    @pl.when(pl.program_id(1) == 0)
    def _():
        h = h_ref[...]
        ms = jnp.mean(h * h, axis=-1, keepdims=True)
        hn_ref[...] = (h * lax.rsqrt(ms + RMS_EPS) * g_ref[...]).astype(BF16)
        o_ref[...] = h

    hn = hn_ref[...]
    gt = _dg(hn, wg_ref[...])
    up = _dg(hn, wu_ref[...])
    act = gt * _sigmoid(gt) * up
    o_ref[...] += _dg(act.astype(BF16), wd_ref[...])


def _ffn(h2d, g, wg, wu, wd, *, tm, tf):
    n, d = h2d.shape
    dff = wg.shape[1]
    return pl.pallas_call(
        _ffn_kernel,
        grid=(n // tm, dff // tf),
        in_specs=[pl.BlockSpec((tm, d), lambda i, j: (i, 0)),
                  pl.BlockSpec((1, d), lambda i, j: (0, 0)),
                  pl.BlockSpec((d, tf), lambda i, j: (0, j)),
                  pl.BlockSpec((d, tf), lambda i, j: (0, j)),
                  pl.BlockSpec((tf, d), lambda i, j: (j, 0))],
        out_specs=pl.BlockSpec((tm, d), lambda i, j: (i, 0)),
        out_shape=jax.ShapeDtypeStruct((n, d), F32),
        scratch_shapes=[pltpu.VMEM((tm, d), BF16)],
        compiler_params=_params("parallel", "arbitrary"),
        name="ffn",
    )(h2d, g.reshape(1, d), wg, wu, wd)


def _pick(n, pref):
    return pref if n % pref == 0 else n


def _block_diag_state(s):
    b, h = s.shape[:2]
    s = s.reshape(b, h // 2, 2, HEAD_B, HEAD_B)
    z = jnp.zeros_like(s[:, :, 0])
    top = jnp.concatenate([s[:, :, 0], z], axis=-1)
    bot = jnp.concatenate([z, s[:, :, 1]], axis=-1)
    return jnp.concatenate([top, bot], axis=-2)


def _unblock_state(s_bd):
    b, g = s_bd.shape[:2]
    s = jnp.stack([s_bd[:, :, :HEAD_B, :HEAD_B], s_bd[:, :, HEAD_B:, HEAD_B:]], axis=2)
    return s.reshape(b, 2 * g, HEAD_B, HEAD_B)


def _layer(x, pos, past, s0, shift0, lam_init, wts):
    b, t, d = x.shape
    n = b * t
    wa = wts["width_a"]
    wb = wts["width_b"]
    x2d = x.reshape(n, d)
    tm = _pick(n, 512)
    rope = _rope_tables(pos, max(t, tm))
    lams = wts["lams"]

    (q,) = _inproj(x2d, wts["norm1_g"], wts["w_q"], [BF16], tm=tm, tn=256,
                   rope=[wts["q_norm_g"]] + rope, scale=HALF_A ** -0.5)
    k32, k16 = _inproj(x2d, wts["norm1_g"], wts["w_k"], [F32, BF16], tm=tm, tn=256,
                       rope=[wts["k_norm_g"]] + rope)
    v32, v16 = _inproj(x2d, wts["norm1_g"], wts["w_v"], [F32, BF16], tm=tm, tn=256)
    (p_rwkv,) = _inproj(x2d, wts["norm1_g"], wts["w_rwkv"], [F32], tm=tm, tn=256)

    q = q.reshape(b, t, wa)
    k16 = k16.reshape(b, t, wa)
    v16 = v16.reshape(b, t, wa)
    if past is None:
        attn = _attn_prompt(q, k16, v16, lams, wts["subln_g"], lam_init, tq=_pick(t, 256))
    else:
        attn = _attn_sample(q, past[0], past[1], k16, v16, lams, wts["subln_g"], lam_init)

    p_rwkv = p_rwkv.reshape(b, t, -1)
    rw, s_bd = _rwkv(p_rwkv, shift0, s0, wts["mu"], wts["vecs"], wts["loras"], tb=_pick(t, 256))

    h = _outproj(x2d, attn.reshape(n, wa), rw.reshape(n, wb), wts["w_out"], tm=tm)
    y = _ffn(h, wts["norm2_g"], wts["w_gate"], wts["w_up"], wts["w_down"], tm=tm, tf=512)

    nh_a = wa // HEAD_A
    rcols = wts["rwkv_cols"]
    return (y.reshape(b, t, d), k32.reshape(b, t, nh_a, HEAD_A), v32.reshape(b, t, nh_a, HEAD_A),
            _unblock_state(s_bd), p_rwkv[:, t - 1:, :rcols])


def kernel(x_prompt, x_sample, cache_attn_k, cache_attn_v, state_rwkv, state_rwkv_shift, norm1_g, w_in, q_norm_g, k_norm_g, lambda_q1, lambda_k1, lambda_q2, lambda_k2, subln_g, mu_rwkv, w0, w2, a0, a2, g2, k_k, k_a, r_k, lnx_g, lnx_b, w_out, norm2_g, w_gate, w_up, w_down):
    depth = w_in.shape[0]
    bp, tp, d = x_prompt.shape
    bs, ts, _ = x_sample.shape
    past_len = cache_attn_k.shape[2]
    nh_a = cache_attn_k.shape[3]
    wa = nh_a * HEAD_A
    wb = d - wa
    rcols = 3 * wb + 3 * LORA
    pad = 2 * LANES - 3 * LORA
    pos_p = jnp.arange(tp, dtype=jnp.int32)
    pos_s = past_len + jnp.arange(ts, dtype=jnp.int32)

    xp, xs = x_prompt, x_sample
    outs = [[] for _ in range(8)]
    for l in range(depth):
        lam_init = 0.8 - 0.6 * math.exp(-0.3 * l)
        wl = w_in[l].astype(BF16)

        def lora_pad(w, slot):
            return jnp.pad(w, ((slot * LORA, 2 * LANES - (slot + 1) * LORA), (0, 0)))

        row = lambda v: v.reshape(1, -1)
        wts = dict(
            width_a=wa, width_b=wb, rwkv_cols=rcols,
            norm1_g=norm1_g[l], norm2_g=norm2_g[l],
            w_q=wl[:, :wa], w_k=wl[:, wa:2 * wa], w_v=wl[:, 2 * wa:3 * wa],
            w_rwkv=jnp.pad(wl[:, 3 * wa:], ((0, 0), (0, pad))),
            q_norm_g=row(q_norm_g[l]), k_norm_g=row(k_norm_g[l]),
            lams=[row(lambda_q1[l]), row(lambda_k1[l]), row(lambda_q2[l]), row(lambda_k2[l])],
            subln_g=row(subln_g[l]),
            mu=jnp.pad(row(mu_rwkv[l]), ((0, 0), (0, pad))),
            vecs=[row(w0[l]), row(a0[l]), row(k_k[l]), row(k_a[l]), row(r_k[l]), row(lnx_g[l]), row(lnx_b[l])],
            loras=[lora_pad(w2[l], 0), lora_pad(a2[l], 1), lora_pad(g2[l], 2)],
            w_out=w_out[l].astype(BF16), w_gate=w_gate[l].astype(BF16),
            w_up=w_up[l].astype(BF16), w_down=w_down[l].astype(BF16),
        )
        s0_p = jnp.zeros((bp, wb // LANES, LANES, LANES), F32)
        shift0_p = jnp.zeros((bp, 1, rcols + pad), F32)
        xp, kp, vp, sp, shp = _layer(xp, pos_p, None, s0_p, shift0_p, lam_init, wts)
        past = (cache_attn_k[l].reshape(bs, past_len, wa), cache_attn_v[l].reshape(bs, past_len, wa))
        shift0_s = jnp.pad(state_rwkv_shift[l], ((0, 0), (0, 0), (0, pad)))
        xs, ks, vs, ss, shs = _layer(xs, pos_s, past, _block_diag_state(state_rwkv[l]), shift0_s,
                                     lam_init, wts)
        for lst, val in zip(outs, (kp, vp, sp, shp, ks, vs, ss, shs)):
            lst.append(val)
    return (xp, xs) + tuple(jnp.stack(o) for o in outs)
```

```python
import functools
import math

import jax
import jax.numpy as jnp
from jax import lax
from jax.experimental import pallas as pl
from jax.experimental.pallas import tpu as pltpu

F32 = jnp.float32
BF16 = jnp.bfloat16

LANES = 128
HEAD_A = 128
HALF_A = 64
ROT_HALF = 8
ROPE_THETA = 500000.0
HEAD_B = 64
CHUNK_MASK = 64
CHUNK_B = 64
LORA = 64
RMS_EPS = 1e-6
GN_EPS = 64e-5
NEG_BIG = -1e30
LOG2E = 1.4426950408889634
MAX_UNSHIFTED_LOGIT2 = 64.0
VMEM_LIMIT = 56 * 1024 * 1024

NN = ((1,), (0,))
NT = ((1,), (1,))


def _dg(a, b, dims=NN):
    return lax.dot_general(a, b, (dims, ((), ())), preferred_element_type=F32)


def _split(a):
    hi = a.astype(BF16)
    lo = (a - hi.astype(F32)).astype(BF16)
    return hi, lo


def _d3(ah, al, bh, bl, dims=NN):
    return _dg(ah, bh, dims) + (_dg(ah, bl, dims) + _dg(al, bh, dims))


def _mm3(a, b, dims=NN):
    return _d3(*_split(a), *_split(b), dims)


def _mm2(a, b_exact):
    ah, al = _split(a)
    return _dg(ah, b_exact) + _dg(al, b_exact)


def _group_matrix(n, group, value):
    r = lax.broadcasted_iota(jnp.int32, (n, n), 0) // group
    c = lax.broadcasted_iota(jnp.int32, (n, n), 1) // group
    return jnp.where(r == c, value, 0.0).astype(BF16)


def _sigmoid(x):
    return 1.0 / (1.0 + jnp.exp(-x))


def _params(*sem):
    return pltpu.CompilerParams(dimension_semantics=sem, vmem_limit_bytes=VMEM_LIMIT)


def _inproj_kernel(*refs, mode, scale):
    if mode == "rope":
        x_ref, g_ref, w_ref, hg_ref, cos_ref, sa_ref, sb_ref = refs[:7]
        outs = refs[7:-1]
    else:
        x_ref, g_ref, w_ref = refs[:3]
        outs = refs[3:-1]
    xn_ref = refs[-1]

    @pl.when(pl.program_id(1) == 0)
    def _():
        x = x_ref[...]
        ms = jnp.mean(x * x, axis=-1, keepdims=True)
        xn_ref[...] = (x * lax.rsqrt(ms + RMS_EPS) * g_ref[...]).astype(BF16)

    acc = _dg(xn_ref[...], w_ref[...])
    if mode == "rope":
        seg = _group_matrix(LANES, HALF_A, 1.0 / HALF_A)
        hg = hg_ref[...] * scale
        cos, sa, sb = cos_ref[...], sa_ref[...], sb_ref[...]
        for g in range(acc.shape[1] // LANES):
            xg = acc[:, g * LANES:(g + 1) * LANES]
            ms = _mm2(xg * xg, seg)
            xg = xg * lax.rsqrt(ms + RMS_EPS) * hg
            up = pltpu.roll(xg, LANES - ROT_HALF, axis=1)
            dn = pltpu.roll(xg, ROT_HALF, axis=1)
            xg = xg * cos + up * sa + dn * sb
            for o in outs:
                o[:, g * LANES:(g + 1) * LANES] = xg.astype(o.dtype)
    else:
        for o in outs:
            o[...] = acc.astype(o.dtype)


def _inproj(x2d, g, w, out_dtypes, *, tm, tn, rope=None, scale=1.0):
    n, d = x2d.shape
    ncol = w.shape[1]
    grid = (n // tm, ncol // tn)
    in_specs = [
        pl.BlockSpec((tm, d), lambda i, j: (i, 0)),
        pl.BlockSpec((1, d), lambda i, j: (0, 0)),
        pl.BlockSpec((d, tn), lambda i, j: (0, j)),
    ]
    args = [x2d, g.reshape(1, d), w]
    if rope is not None:
        hg, cos, sa, sb = rope
        nblk = cos.shape[0] // tm
        in_specs.append(pl.BlockSpec((1, LANES), lambda i, j: (0, 0)))
        args.append(hg)
        for t in (cos, sa, sb):
            in_specs.append(pl.BlockSpec((tm, LANES), lambda i, j: (i % nblk, 0)))
            args.append(t)
    out_shape = [jax.ShapeDtypeStruct((n, ncol), dt) for dt in out_dtypes]
    out_specs = [pl.BlockSpec((tm, tn), lambda i, j: (i, j)) for _ in out_dtypes]
    return pl.pallas_call(
        functools.partial(_inproj_kernel, mode="rope" if rope is not None else "plain", scale=scale),
        grid=grid, in_specs=in_specs, out_specs=out_specs, out_shape=out_shape,
        scratch_shapes=[pltpu.VMEM((tm, d), BF16)],
        compiler_params=_params("parallel", "arbitrary"),
        name="inproj_rope" if rope is not None else "inproj",
    )(*args)


def _rope_tables(pos, rows):
    inv = jnp.float32(ROPE_THETA) ** (-jnp.arange(0, 2 * ROT_HALF, 2, dtype=F32) / (2 * ROT_HALF))
    ang = pos.astype(F32)[:, None] * inv
    cos8, sin8 = jnp.cos(ang), jnp.sin(ang)
    t = pos.shape[0]
    one = jnp.ones((t, HALF_A - 2 * ROT_HALF), F32)
    zero8 = jnp.zeros((t, ROT_HALF), F32)
    zrest = jnp.zeros((t, HALF_A - 2 * ROT_HALF), F32)
    cos_h = jnp.concatenate([cos8, cos8, one], axis=1)
    sa_h = jnp.concatenate([-sin8, zero8, zrest], axis=1)
    sb_h = jnp.concatenate([zero8, sin8, zrest], axis=1)
    tabs = [jnp.concatenate([h, h], axis=1) for h in (cos_h, sa_h, sb_h)]
    if rows > t:
        tabs = [jnp.tile(h, (rows // t, 1)) for h in tabs]
    return tabs


def _stack_halves(q):
    lane = lax.broadcasted_iota(jnp.int32, q.shape, 1)
    zero = jnp.zeros_like(q)
    return jnp.concatenate([jnp.where(lane < HALF_A, q, zero), jnp.where(lane < HALF_A, zero, q)], axis=0)


def _lambda(lq1, lk1, lq2, lk2, lam_init):
    return (jnp.exp(jnp.sum(lq1 * lk1, axis=-1, keepdims=True))
            - jnp.exp(jnp.sum(lq2 * lk2, axis=-1, keepdims=True)) + lam_init)


def _attn_finish(num1, den1, num2, den2, lam, g, lam_init):
    o = num1 / den1 - lam * (num2 / den2)
    ms = jnp.mean(o * o, axis=-1, keepdims=True)
    return o * lax.rsqrt(ms + RMS_EPS) * (g * (1.0 - lam_init))


def _chunk_mask(shape, tq):
    row = lax.broadcasted_iota(jnp.int32, shape, 0)
    col = lax.broadcasted_iota(jnp.int32, shape, 1)
    qc = jnp.where(row >= tq, row - tq, row) // CHUNK_MASK
    return col // CHUNK_MASK <= qc


def _attn_prompt_kernel(q_ref, k_ref, v_ref, lq1, lk1, lq2, lk2, g_ref, o_ref, acc_ref, *, tq, tk, lam_init):
    i = pl.program_id(2)
    qs = _stack_halves(q_ref[...])
    ones = jnp.ones((tk, HEAD_A), BF16)
    nfull = (i * tq) // tk

    def tile(j, first_masked_chunk):
        rows = pl.ds(pl.multiple_of(j * tk, tk), tk)
        p = jnp.exp2(_dg(qs, k_ref[rows, :], NT))
        if first_masked_chunk is not None:
            row = lax.broadcasted_iota(jnp.int32, p.shape, 0)
            col = lax.broadcasted_iota(jnp.int32, p.shape, 1)
            qc = jnp.where(row >= tq, row - tq, row) // CHUNK_MASK + first_masked_chunk
            p = jnp.where(col // CHUNK_MASK <= qc, p, 0.0)
        ve = jnp.concatenate([v_ref[rows, :], ones], axis=1)
        return _dg(p.astype(BF16), ve)

    acc_ref[...] = tile(nfull, (i * tq - nfull * tk) // CHUNK_MASK)

    def body(j, c):
        acc_ref[...] += tile(j, None)
        return c

    lax.fori_loop(0, nfull, body, 0)
    acc = acc_ref[...]
    lam = _lambda(lq1[...], lk1[...], lq2[...], lk2[...], lam_init)
    o = _attn_finish(acc[:tq, :HEAD_A], acc[:tq, HEAD_A:], acc[tq:, :HEAD_A], acc[tq:, HEAD_A:],
                     lam, g_ref[...], lam_init)
    o_ref[...] = o.astype(o_ref.dtype)


def _attn_prompt_safe_kernel(q_ref, k_ref, v_ref, lq1, lk1, lq2, lk2, g_ref, o_ref, *, tq, lam_init):
    i = pl.program_id(2)
    qs = _stack_halves(q_ref[...])

    def tile(j, carry, masked):
        m, l, acc = carry
        start = pl.multiple_of(j * tq, tq)
        k = k_ref[pl.ds(start, tq), :]
        v = v_ref[pl.ds(start, tq), :]
        s = _dg(qs, k, NT)
        if masked:
            s = jnp.where(_chunk_mask(s.shape, tq), s, NEG_BIG)
        m_new = jnp.maximum(m, jnp.max(s, axis=-1, keepdims=True))
        p = jnp.exp2(s - m_new)
        alpha = jnp.exp2(m - m_new)
        l = alpha * l + jnp.sum(p, axis=-1, keepdims=True)
        acc = alpha * acc + _dg(p.astype(BF16), v)
        return m_new, l, acc

    init = (jnp.full((2 * tq, 1), NEG_BIG, F32), jnp.zeros((2 * tq, 1), F32),
            jnp.zeros((2 * tq, HEAD_A), F32))
    carry = lax.fori_loop(0, i, lambda j, c: tile(j, c, False), init)
    _, l, acc = tile(i, carry, True)
    lam = _lambda(lq1[...], lk1[...], lq2[...], lk2[...], lam_init)
    o = _attn_finish(acc[:tq], l[:tq], acc[tq:], l[tq:], lam, g_ref[...], lam_init)
    o_ref[...] = o.astype(o_ref.dtype)


def _attn_prompt(q, k, v, lams, g, lam_init, logit_bound, *, tq, tk):
    b, t, width = q.shape
    nh = width // HEAD_A
    small = pl.BlockSpec((1, HALF_A), lambda bi, h, i: (0, 0))

    def call(body, scratch, name, **kw):
        return pl.pallas_call(
            functools.partial(body, tq=tq, lam_init=lam_init, **kw),
            grid=(b, nh, t // tq),
            in_specs=[
                pl.BlockSpec((None, tq, HEAD_A), lambda bi, h, i: (bi, i, h)),
                pl.BlockSpec((None, t, HEAD_A), lambda bi, h, i: (bi, 0, h)),
                pl.BlockSpec((None, t, HEAD_A), lambda bi, h, i: (bi, 0, h)),
                small, small, small, small,
                pl.BlockSpec((1, HEAD_A), lambda bi, h, i: (0, 0)),
            ],
            out_specs=pl.BlockSpec((None, tq, HEAD_A), lambda bi, h, i: (bi, i, h)),
            out_shape=jax.ShapeDtypeStruct((b, t, width), BF16),
            scratch_shapes=scratch,
            compiler_params=_params("parallel", "parallel", "arbitrary"),
            name=name,
        )

    fast = call(_attn_prompt_kernel, [pltpu.VMEM((2 * tq, 2 * HEAD_A), F32)], "attn_prompt", tk=tk)
    safe = call(_attn_prompt_safe_kernel, [], "attn_prompt_safe")
    return lax.cond(logit_bound <= MAX_UNSHIFTED_LOGIT2, fast, safe, q, k, v, *lams, g)


def _attn_sample_kernel(q_ref, kp_ref, vp_ref, kn_ref, vn_ref, lq1, lk1, lq2, lk2, g_ref, o_ref,
                        *, tq, past, lam_init):
    qs = _stack_halves(q_ref[...])
    s_p = _dg(qs, kp_ref[...].astype(BF16), NT)
    s_n = _dg(qs, kn_ref[...], NT)
    row = lax.broadcasted_iota(jnp.int32, s_n.shape, 0)
    col = lax.broadcasted_iota(jnp.int32, s_n.shape, 1)
    qc = (past + jnp.where(row >= tq, row - tq, row)) // CHUNK_MASK
    s_n = jnp.where((past + col) // CHUNK_MASK <= qc, s_n, NEG_BIG)
    m = jnp.maximum(jnp.max(s_p, axis=-1, keepdims=True), jnp.max(s_n, axis=-1, keepdims=True))
    p_p = jnp.exp2(s_p - m)
    p_n = jnp.exp2(s_n - m)
    l = jnp.sum(p_p, axis=-1, keepdims=True) + jnp.sum(p_n, axis=-1, keepdims=True)
    acc = _dg(p_p.astype(BF16), vp_ref[...].astype(BF16)) + _dg(p_n.astype(BF16), vn_ref[...])
    lam = _lambda(lq1[...], lk1[...], lq2[...], lk2[...], lam_init)
    o = _attn_finish(acc[:tq], l[:tq], acc[tq:], l[tq:], lam, g_ref[...], lam_init)
    o_ref[...] = o.astype(o_ref.dtype)


def _attn_sample(q, k_past, v_past, k_new, v_new, lams, g, lam_init):
    b, t, width = q.shape
    past = k_past.shape[1]
    nh = width // HEAD_A
    small = pl.BlockSpec((1, HALF_A), lambda bi, h: (0, 0))
    new = pl.BlockSpec((None, t, HEAD_A), lambda bi, h: (bi, 0, h))
    old = pl.BlockSpec((None, past, HEAD_A), lambda bi, h: (bi, 0, h))
    return pl.pallas_call(
        functools.partial(_attn_sample_kernel, tq=t, past=past, lam_init=lam_init),
        grid=(b, nh),
        in_specs=[new, old, old, new, new, small, small, small, small,
                  pl.BlockSpec((1, HEAD_A), lambda bi, h: (0, 0))],
        out_specs=new,
        out_shape=jax.ShapeDtypeStruct((b, t, width), BF16),
        compiler_params=_params("parallel", "parallel"),
        name="attn_sample",
    )(q, k_past, v_past, k_new, v_new, *lams, g)


def _stack_heads(x):
    lane = lax.broadcasted_iota(jnp.int32, x.shape, 1)
    return jnp.concatenate([jnp.where(lane < HEAD_B, x, 0.0), jnp.where(lane < HEAD_B, 0.0, x)], axis=0)


def _rwkv_kernel(pr_ref, pk_ref, pv_ref, pl_ref, sr_ref, sk_ref, sv_ref, sl_ref,
                 mr_ref, mk_ref, mv_ref, ml_ref,
                 w0_ref, a0_ref, kk_ref, ka_ref, rk_ref, lg_ref, lb_ref,
                 w2_ref, a2_ref, g2_ref, s0_ref,
                 out_ref, sout_ref,
                 s_sc, cr_sc, ck_sc, cv_sc, cl_sc, *, tb, ng):
    t = pl.program_id(2)
    width = ng * LANES

    @pl.when(t == 0)
    def _():
        s_sc[...] = s0_ref[...]
        cr_sc[...] = sr_ref[...]
        ck_sc[...] = sk_ref[...]
        cv_sc[...] = sv_ref[...]
        cl_sc[...] = sl_ref[...]

    def shifted(p_ref, carry_ref, mu_ref):
        p = p_ref[...]
        prev = pltpu.roll(p, 1, axis=0)
        row = lax.broadcasted_iota(jnp.int32, p.shape, 0)
        prev = jnp.where(row == 0, carry_ref[...], prev)
        carry_ref[...] = p[tb - 1:tb, :]
        return p + (prev - p) * mu_ref[...]

    xr = shifted(pr_ref, cr_sc, mr_ref)
    xk = shifted(pk_ref, ck_sc, mk_ref)
    xv = shifted(pv_ref, cv_sc, mv_ref)
    xl = shifted(pl_ref, cl_sc, ml_ref)

    seg = _group_matrix(width, HEAD_B, 1.0)
    z = w0_ref[...] + _mm3(jnp.tanh(xl), w2_ref[...])
    nz = -z
    softplus = jnp.maximum(nz, 0.0) + jnp.log(1.0 + jnp.exp(-jnp.abs(nz)))
    logw = -jnp.exp(-softplus - 0.5)
    ag = _sigmoid(a0_ref[...] + _mm3(xl, a2_ref[...]))
    gate = _mm3(_sigmoid(xl), g2_ref[...])
    kk = xk * kk_ref[...]
    kk = kk / jnp.maximum(jnp.sqrt(_mm2(kk * kk, seg)), 1e-12)
    km = xk * (1.0 + (ag - 1.0) * ka_ref[...])
    bonus = _mm2(xr * km * rk_ref[...], seg) * xv

    c = CHUNK_B
    c2 = 2 * c
    ri = lax.broadcasted_iota(jnp.int32, (c, c), 0)
    ci = lax.broadcasted_iota(jnp.int32, (c, c), 1)
    tri = jnp.where(ci <= ri, 1.0, 0.0).astype(BF16)
    r2 = lax.broadcasted_iota(jnp.int32, (c2, c2), 0)
    c2i = lax.broadcasted_iota(jnp.int32, (c2, c2), 1)
    rm = r2 % c
    cm = c2i % c
    strict = cm < rm
    incl = cm <= rm
    eye = jnp.where(r2 == c2i, 1.0, 0.0)

    def off_blocks(s):
        rs = rm // s
        return (rs - cm // s) * 2 + rs % 2 == 3

    nchunk = tb // c
    items = [(g, n) for n in range(nchunk) for g in range(ng)]

    pre = {}
    for g, n in items:
        rows = slice(n * c, (n + 1) * c)
        cols = slice(g * LANES, (g + 1) * LANES)
        lw = logw[rows, cols]
        h1 = lw.astype(BF16)
        r1 = lw - h1.astype(F32)
        h2 = r1.astype(BF16)
        h3 = (r1 - h2.astype(F32)).astype(BF16)
        cum = _dg(tri, h1) + (_dg(tri, h2) + _dg(tri, h3))
        w_in = jnp.exp(cum)
        w_ex = jnp.exp(cum - lw)
        w_inv = jnp.exp(-cum)
        kc = kk[rows, cols]
        a_s = _stack_heads(-kc * w_ex)
        r_h = _stack_heads(xr[rows, cols] * w_in).astype(BF16)
        bk = jnp.concatenate([_stack_heads(kc * ag[rows, cols] * w_inv),
                              _stack_heads(km[rows, cols] * w_inv)], axis=0)
        v_s = _stack_heads(xv[rows, cols])
        pre[g, n] = dict(a_s=a_s, a_sp=_split(a_s), r_h=r_h, bk=_split(bk), v_s=v_s, v_sp=_split(v_s),
                         w_last=w_in[c - 1:c, :])

    for key in items:
        d = pre[key]
        g_a = _d3(*d["a_sp"], *d["bk"], NT)
        g_r = _dg(d["r_h"], d["bk"][0], NT)
        d["n_ab"] = jnp.where(strict, g_a[:, :c2], 0.0)
        n_ak = jnp.where(strict, g_a[:, c2:], 0.0)
        d["n_rb"] = jnp.where(incl, g_r[:, :c2], 0.0).astype(BF16)
        n_rk = jnp.where(incl, g_r[:, c2:], 0.0).astype(BF16)
        d["akv"] = _d3(*_split(n_ak), *d["v_sp"])
        d["yv"] = _dg(n_rk, d["v_sp"][0])
        d["vk"] = _d3(*_split(d["v_s"].T), d["bk"][0][c2:], d["bk"][1][c2:])

    for key in items:
        d = pre[key]
        d["tinv"] = eye + jnp.where(off_blocks(1), d["n_ab"], 0.0)
    s = 2
    while s < c:
        sel = off_blocks(s)
        for key in items:
            d = pre[key]
            d["tmp"] = _mm3(jnp.where(sel, d["n_ab"], 0.0), d["tinv"])
        for key in items:
            d = pre[key]
            d["tinv"] = d["tinv"] + _mm3(d["tinv"], d["tmp"])
        s *= 2

    for key in items:
        d = pre[key]
        tx = _mm3(d["tinv"], jnp.concatenate([d["a_s"], d["akv"]], axis=1))
        d["ta"] = _split(tx[:, :c2])
        d["u0"] = tx[:, c2:]
        d["u0t"] = tx[:, c2:].T

    state = [s_sc[g] for g in range(ng)]
    ys = {}
    for n in range(nchunk):
        for g in range(ng):
            d = pre[g, n]
            s_h, s_l = _split(state[g])
            ut = _d3(s_h, s_l, *d["ta"], NT) + d["u0t"]
            u1 = _dg(d["ta"][0], s_h, NT) + d["u0"]
            y_s = _dg(d["r_h"], s_h, NT) + _dg(d["n_rb"], u1.astype(BF16)) + d["yv"]
            ys[g, n] = y_s[:c] + y_s[c:]
            upd = _d3(*_split(ut), d["bk"][0][:c2], d["bk"][1][:c2])
            state[g] = (state[g] + upd + d["vk"]) * d["w_last"]
    for g in range(ng):
        s_sc[g] = state[g]

    @pl.when(t == pl.num_programs(2) - 1)
    def _():
        for g in range(ng):
            sout_ref[g] = state[g]

    y = jnp.concatenate(
        [jnp.concatenate([ys[g, n] for n in range(nchunk)], axis=0) for g in range(ng)], axis=1)
    segm = _group_matrix(width, HEAD_B, 1.0 / HEAD_B)
    mean = _mm2(y, segm)
    dlt = y - mean
    var = _mm2(dlt * dlt, segm)
    yn = dlt * lax.rsqrt(var + GN_EPS) * lg_ref[...] + lb_ref[...]
    out_ref[...] = ((yn + bonus) * gate).astype(out_ref.dtype)


def _rwkv(p, shift0, s0_bd, mu, vecs, loras, *, tb, ng):
    b, t, _ = p.shape
    width = vecs[0].shape[-1]
    wblk = ng * LANES
    nblk = width // wblk
    lcol = 3 * width // (2 * LANES)

    def col(off):
        return lambda bi, g, ti: (bi, ti, off + g)

    def row(off):
        return lambda bi, g, ti: (bi, 0, off + g)

    def par(off):
        return lambda bi, g, ti: (0, off + g)

    in_specs = (
        [pl.BlockSpec((None, tb, wblk), col(k * nblk)) for k in range(3)]
        + [pl.BlockSpec((None, tb, 2 * LANES), lambda bi, g, ti: (bi, ti, lcol))]
        + [pl.BlockSpec((None, 1, wblk), row(k * nblk)) for k in range(3)]
        + [pl.BlockSpec((None, 1, 2 * LANES), lambda bi, g, ti: (bi, 0, lcol))]
        + [pl.BlockSpec((1, wblk), par(k * nblk)) for k in range(3)]
        + [pl.BlockSpec((1, 2 * LANES), lambda bi, g, ti: (0, lcol))]
        + [pl.BlockSpec((1, wblk), par(0)) for _ in vecs]
        + [pl.BlockSpec((2 * LANES, wblk), par(0)) for _ in loras]
        + [pl.BlockSpec((None, ng, LANES, LANES), lambda bi, g, ti: (bi, g, 0, 0))]
    )
    return pl.pallas_call(
        functools.partial(_rwkv_kernel, tb=tb, ng=ng),
        grid=(b, nblk, t // tb),
        in_specs=in_specs,
        out_specs=[pl.BlockSpec((None, tb, wblk), col(0)),
                   pl.BlockSpec((None, ng, LANES, LANES), lambda bi, g, ti: (bi, g, 0, 0))],
        out_shape=[jax.ShapeDtypeStruct((b, t, width), BF16),
                   jax.ShapeDtypeStruct((b, width // LANES, LANES, LANES), F32)],
        scratch_shapes=[pltpu.VMEM((ng, LANES, LANES), F32), pltpu.VMEM((1, wblk), F32),
                        pltpu.VMEM((1, wblk), F32), pltpu.VMEM((1, wblk), F32),
                        pltpu.VMEM((1, 2 * LANES), F32)],
        compiler_params=_params("parallel", "parallel", "arbitrary"),
        name="rwkv",
    )(p, p, p, p, shift0, shift0, shift0, shift0, mu, mu, mu, mu, *vecs, *loras, s0_bd)


def _outproj_kernel(x_ref, a_ref, b_ref, wa_ref, wb_ref, o_ref):
    o_ref[...] = x_ref[...] + (_dg(a_ref[...], wa_ref[...]) + _dg(b_ref[...], wb_ref[...]))


def _outproj(x2d, a, b, w, *, tm):
    n, d = x2d.shape
    half = a.shape[1]
    return pl.pallas_call(
        _outproj_kernel,
        grid=(n // tm,),
        in_specs=[pl.BlockSpec((tm, d), lambda i: (i, 0)),
                  pl.BlockSpec((tm, half), lambda i: (i, 0)),
                  pl.BlockSpec((tm, half), lambda i: (i, 0)),
                  pl.BlockSpec((half, d), lambda i: (0, 0)),
                  pl.BlockSpec((half, d), lambda i: (1, 0))],
        out_specs=pl.BlockSpec((tm, d), lambda i: (i, 0)),
        out_shape=jax.ShapeDtypeStruct((n, d), F32),
        compiler_params=_params("parallel"),
        name="outproj",
    )(x2d, a, b, w, w)


def _ffn_kernel(h_ref, g_ref, wg_ref, wu_ref, wd_ref, o_ref, hn_ref):
    @pl.when(pl.program_id(1) == 0)
    def _():
        h = h_ref[...]
        ms = jnp.mean(h * h, axis=-1, keepdims=True)
        hn_ref[...] = (h * lax.rsqrt(ms + RMS_EPS) * g_ref[...]).astype(BF16)
        o_ref[...] = h

    hn = hn_ref[...]
    gt = _dg(hn, wg_ref[...])
    up = _dg(hn, wu_ref[...])
    act = gt * _sigmoid(gt) * up
    o_ref[...] += _dg(act.astype(BF16), wd_ref[...])


def _ffn(h2d, g, wg, wu, wd, *, tm, tf):
    n, d = h2d.shape
    dff = wg.shape[1]
    return pl.pallas_call(
        _ffn_kernel,
        grid=(n // tm, dff // tf),
        in_specs=[pl.BlockSpec((tm, d), lambda i, j: (i, 0)),
                  pl.BlockSpec((1, d), lambda i, j: (0, 0)),
                  pl.BlockSpec((d, tf), lambda i, j: (0, j)),
                  pl.BlockSpec((d, tf), lambda i, j: (0, j)),
                  pl.BlockSpec((tf, d), lambda i, j: (j, 0))],
        out_specs=pl.BlockSpec((tm, d), lambda i, j: (i, 0)),
        out_shape=jax.ShapeDtypeStruct((n, d), F32),
        scratch_shapes=[pltpu.VMEM((tm, d), BF16)],
        compiler_params=_params("parallel", "arbitrary"),
        name="ffn",
    )(h2d, g.reshape(1, d), wg, wu, wd)


def _pick(n, pref):
    return pref if n % pref == 0 else n


def _block_diag_state(s):
    b, h = s.shape[:2]
    s = s.reshape(b, h // 2, 2, HEAD_B, HEAD_B)
    z = jnp.zeros_like(s[:, :, 0])
    top = jnp.concatenate([s[:, :, 0], z], axis=-1)
    bot = jnp.concatenate([z, s[:, :, 1]], axis=-1)
    return jnp.concatenate([top, bot], axis=-2)


def _unblock_state(s_bd):
    b, g = s_bd.shape[:2]
    s = jnp.stack([s_bd[:, :, :HEAD_B, :HEAD_B], s_bd[:, :, HEAD_B:, HEAD_B:]], axis=2)
    return s.reshape(b, 2 * g, HEAD_B, HEAD_B)


def _layer(x, pos, past, s0, shift0, lam_init, wts):
    b, t, d = x.shape
    n = b * t
    wa = wts["width_a"]
    wb = wts["width_b"]
    x2d = x.reshape(n, d)
    tm = _pick(n, 512)
    rope = _rope_tables(pos, max(t, tm))
    lams = wts["lams"]

    (q,) = _inproj(x2d, wts["norm1_g"], wts["w_q"], [BF16], tm=tm, tn=256,
                   rope=[wts["q_norm_g"]] + rope, scale=HALF_A ** -0.5 * LOG2E)
    k32, k16 = _inproj(x2d, wts["norm1_g"], wts["w_k"], [F32, BF16], tm=tm, tn=256,
                       rope=[wts["k_norm_g"]] + rope)
    v32, v16 = _inproj(x2d, wts["norm1_g"], wts["w_v"], [F32, BF16], tm=tm, tn=256)
    (p_rwkv,) = _inproj(x2d, wts["norm1_g"], wts["w_rwkv"], [F32], tm=tm, tn=256)

    q = q.reshape(b, t, wa)
    k16 = k16.reshape(b, t, wa)
    v16 = v16.reshape(b, t, wa)
    if past is None:
        bound = (8.0 * 1.01 * LOG2E) * jnp.max(jnp.abs(wts["q_norm_g"])) * jnp.max(jnp.abs(wts["k_norm_g"]))
        attn = _attn_prompt(q, k16, v16, lams, wts["subln_g"], lam_init, bound,
                            tq=_pick(t, 512), tk=_pick(t, 1024))
        tb, ng = _pick(t, 256), 2
    else:
        attn = _attn_sample(q, past[0], past[1], k16, v16, lams, wts["subln_g"], lam_init)
        tb, ng = _pick(t, CHUNK_B), wb // LANES

    p_rwkv = p_rwkv.reshape(b, t, -1)
    rw, s_bd = _rwkv(p_rwkv, shift0, s0, wts["mu"], wts["vecs"], wts["loras"], tb=tb, ng=ng)

    h = _outproj(x2d, attn.reshape(n, wa), rw.reshape(n, wb), wts["w_out"], tm=tm)
    y = _ffn(h, wts["norm2_g"], wts["w_gate"], wts["w_up"], wts["w_down"], tm=tm, tf=512)

    nh_a = wa // HEAD_A
    rcols = wts["rwkv_cols"]
    return (y.reshape(b, t, d), k32.reshape(b, t, nh_a, HEAD_A), v32.reshape(b, t, nh_a, HEAD_A),
            _unblock_state(s_bd), p_rwkv[:, t - 1:, :rcols])


def kernel(x_prompt, x_sample, cache_attn_k, cache_attn_v, state_rwkv, state_rwkv_shift, norm1_g, w_in, q_norm_g, k_norm_g, lambda_q1, lambda_k1, lambda_q2, lambda_k2, subln_g, mu_rwkv, w0, w2, a0, a2, g2, k_k, k_a, r_k, lnx_g, lnx_b, w_out, norm2_g, w_gate, w_up, w_down):
    depth = w_in.shape[0]
    bp, tp, d = x_prompt.shape
    bs, ts, _ = x_sample.shape
    past_len = cache_attn_k.shape[2]
    nh_a = cache_attn_k.shape[3]
    wa = nh_a * HEAD_A
    wb = d - wa
    rcols = 3 * wb + 3 * LORA
    pad = 2 * LANES - 3 * LORA
    pos_p = jnp.arange(tp, dtype=jnp.int32)
    pos_s = past_len + jnp.arange(ts, dtype=jnp.int32)

    xp, xs = x_prompt, x_sample
    outs = [[] for _ in range(8)]
    for l in range(depth):
        lam_init = 0.8 - 0.6 * math.exp(-0.3 * l)
        wl = w_in[l].astype(BF16)

        def lora_pad(w, slot):
            return jnp.pad(w, ((slot * LORA, 2 * LANES - (slot + 1) * LORA), (0, 0)))

        row = lambda v: v.reshape(1, -1)
        wts = dict(
            width_a=wa, width_b=wb, rwkv_cols=rcols,
            norm1_g=norm1_g[l], norm2_g=norm2_g[l],
            w_q=wl[:, :wa], w_k=wl[:, wa:2 * wa], w_v=wl[:, 2 * wa:3 * wa],
            w_rwkv=jnp.pad(wl[:, 3 * wa:], ((0, 0), (0, pad))),
            q_norm_g=row(q_norm_g[l]), k_norm_g=row(k_norm_g[l]),
            lams=[row(lambda_q1[l]), row(lambda_k1[l]), row(lambda_q2[l]), row(lambda_k2[l])],
            subln_g=row(subln_g[l]),
            mu=jnp.pad(row(mu_rwkv[l]), ((0, 0), (0, pad))),
            vecs=[row(w0[l]), row(a0[l]), row(k_k[l]), row(k_a[l]), row(r_k[l]), row(lnx_g[l]), row(lnx_b[l])],
            loras=[lora_pad(w2[l], 0), lora_pad(a2[l], 1), lora_pad(g2[l], 2)],
            w_out=w_out[l].astype(BF16), w_gate=w_gate[l].astype(BF16),
            w_up=w_up[l].astype(BF16), w_down=w_down[l].astype(BF16),
        )
        s0_p = jnp.zeros((bp, wb // LANES, LANES, LANES), F32)
        shift0_p = jnp.zeros((bp, 1, rcols + pad), F32)
        xp, kp, vp, sp, shp = _layer(xp, pos_p, None, s0_p, shift0_p, lam_init, wts)
        past = (cache_attn_k[l].reshape(bs, past_len, wa), cache_attn_v[l].reshape(bs, past_len, wa))
        shift0_s = jnp.pad(state_rwkv_shift[l], ((0, 0), (0, 0), (0, pad)))
        xs, ks, vs, ss, shs = _layer(xs, pos_s, past, _block_diag_state(state_rwkv[l]), shift0_s,
                                     lam_init, wts)
        for lst, val in zip(outs, (kp, vp, sp, shp, ks, vs, ss, shs)):
            lst.append(val)
    return (xp, xs) + tuple(jnp.stack(o) for o in outs)
```

```python
import functools
import math

import jax
import jax.numpy as jnp
from jax import lax
from jax.experimental import pallas as pl
from jax.experimental.pallas import tpu as pltpu

F32 = jnp.float32
BF16 = jnp.bfloat16

LANES = 128
HEAD_A = 128
HALF_A = 64
ROT_HALF = 8
ROPE_THETA = 500000.0
HEAD_B = 64
CHUNK_MASK = 64
CHUNK_B = 64
LORA = 64
RMS_EPS = 1e-6
GN_EPS = 64e-5
NEG_BIG = -1e30
LOG2E = 1.4426950408889634
MAX_UNSHIFTED_LOGIT2 = 64.0
VMEM_LIMIT = 56 * 1024 * 1024

NN = ((1,), (0,))
NT = ((1,), (1,))


def _dg(a, b, dims=NN):
    return lax.dot_general(a, b, (dims, ((), ())), preferred_element_type=F32)


def _split(a):
    hi = a.astype(BF16)
    lo = (a - hi.astype(F32)).astype(BF16)
    return hi, lo


def _d3(ah, al, bh, bl, dims=NN):
    return _dg(ah, bh, dims) + (_dg(ah, bl, dims) + _dg(al, bh, dims))


def _mm3(a, b, dims=NN):
    return _d3(*_split(a), *_split(b), dims)


def _mm2(a, b_exact):
    ah, al = _split(a)
    return _dg(ah, b_exact) + _dg(al, b_exact)


def _group_matrix(n, group, value):
    r = lax.broadcasted_iota(jnp.int32, (n, n), 0) // group
    c = lax.broadcasted_iota(jnp.int32, (n, n), 1) // group
    return jnp.where(r == c, value, 0.0).astype(BF16)


def _sigmoid(x):
    return 1.0 / (1.0 + jnp.exp(-x))


def _params(*sem):
    return pltpu.CompilerParams(dimension_semantics=sem, vmem_limit_bytes=VMEM_LIMIT)


def _inproj_kernel(x_ref, g_ref, w_ref, qg_ref, kg_ref, cos_ref, sa_ref, sb_ref,
                   q_ref, k32_ref, k16_ref, v32_ref, v16_ref, p_ref, xn_ref, *, nsec, q_scale):
    j = pl.program_id(1)

    @pl.when(j == 0)
    def _():
        x = x_ref[...]
        ms = jnp.mean(x * x, axis=-1, keepdims=True)
        xn_ref[...] = (x * lax.rsqrt(ms + RMS_EPS) * g_ref[...]).astype(BF16)

    acc = _dg(xn_ref[...], w_ref[...])

    @pl.when(j < 2 * nsec)
    def _():
        is_q = j < nsec
        seg = _group_matrix(LANES, HALF_A, 1.0 / HALF_A)
        hg = jnp.where(is_q, qg_ref[...] * q_scale, kg_ref[...])
        cos, sa, sb = cos_ref[...], sa_ref[...], sb_ref[...]
        groups = []
        for g in range(acc.shape[1] // LANES):
            xg = acc[:, g * LANES:(g + 1) * LANES]
            ms = _mm2(xg * xg, seg)
            xg = xg * lax.rsqrt(ms + RMS_EPS) * hg
            up = pltpu.roll(xg, LANES - ROT_HALF, axis=1)
            dn = pltpu.roll(xg, ROT_HALF, axis=1)
            groups.append(xg * cos + up * sa + dn * sb)
        rot = jnp.concatenate(groups, axis=1)

        @pl.when(is_q)
        def _():
            q_ref[...] = rot.astype(BF16)

        @pl.when(jnp.logical_not(is_q))
        def _():
            k32_ref[...] = rot
            k16_ref[...] = rot.astype(BF16)

    @pl.when(jnp.logical_and(j >= 2 * nsec, j < 3 * nsec))
    def _():
        v32_ref[...] = acc
        v16_ref[...] = acc.astype(BF16)

    @pl.when(j >= 3 * nsec)
    def _():
        p_ref[...] = acc


def _inproj(x2d, g, w, qg, kg, rope, *, width_a, q_scale, tm, tn):
    n, d = x2d.shape
    ncol = w.shape[1]
    nsec = width_a // tn
    nblk = rope[0].shape[0] // tm
    sec = lambda k: (lambda i, j: (i, jnp.clip(j - k * nsec, 0, nsec - 1)))
    vec = pl.BlockSpec((1, LANES), lambda i, j: (0, 0))
    tab = pl.BlockSpec((tm, LANES), lambda i, j: (i % nblk, 0))
    blk = lambda imap: pl.BlockSpec((tm, tn), imap)
    return pl.pallas_call(
        functools.partial(_inproj_kernel, nsec=nsec, q_scale=q_scale),
        grid=(n // tm, ncol // tn),
        in_specs=[pl.BlockSpec((tm, d), lambda i, j: (i, 0)),
                  pl.BlockSpec((1, d), lambda i, j: (0, 0)),
                  pl.BlockSpec((d, tn), lambda i, j: (0, j)),
                  vec, vec, tab, tab, tab],
        out_specs=[blk(sec(0)), blk(sec(1)), blk(sec(1)), blk(sec(2)), blk(sec(2)),
                   blk(lambda i, j: (i, jnp.maximum(j - 3 * nsec, 0)))],
        out_shape=[jax.ShapeDtypeStruct((n, width_a), BF16),
                   jax.ShapeDtypeStruct((n, width_a), F32), jax.ShapeDtypeStruct((n, width_a), BF16),
                   jax.ShapeDtypeStruct((n, width_a), F32), jax.ShapeDtypeStruct((n, width_a), BF16),
                   jax.ShapeDtypeStruct((n, ncol - 3 * width_a), F32)],
        scratch_shapes=[pltpu.VMEM((tm, d), BF16)],
        compiler_params=_params("parallel", "arbitrary"),
        name="inproj",
    )(x2d, g.reshape(1, d), w, qg, kg, *rope)


def _rope_tables(pos, rows):
    inv = jnp.float32(ROPE_THETA) ** (-jnp.arange(0, 2 * ROT_HALF, 2, dtype=F32) / (2 * ROT_HALF))
    ang = pos.astype(F32)[:, None] * inv
    cos8, sin8 = jnp.cos(ang), jnp.sin(ang)
    t = pos.shape[0]
    one = jnp.ones((t, HALF_A - 2 * ROT_HALF), F32)
    zero8 = jnp.zeros((t, ROT_HALF), F32)
    zrest = jnp.zeros((t, HALF_A - 2 * ROT_HALF), F32)
    cos_h = jnp.concatenate([cos8, cos8, one], axis=1)
    sa_h = jnp.concatenate([-sin8, zero8, zrest], axis=1)
    sb_h = jnp.concatenate([zero8, sin8, zrest], axis=1)
    tabs = [jnp.concatenate([h, h], axis=1) for h in (cos_h, sa_h, sb_h)]
    if rows > t:
        tabs = [jnp.tile(h, (rows // t, 1)) for h in tabs]
    return tabs


def _stack_halves(q):
    lane = lax.broadcasted_iota(jnp.int32, q.shape, 1)
    zero = jnp.zeros_like(q)
    return jnp.concatenate([jnp.where(lane < HALF_A, q, zero), jnp.where(lane < HALF_A, zero, q)], axis=0)


def _lambda(lq1, lk1, lq2, lk2, lam_init):
    return (jnp.exp(jnp.sum(lq1 * lk1, axis=-1, keepdims=True))
            - jnp.exp(jnp.sum(lq2 * lk2, axis=-1, keepdims=True)) + lam_init)


def _attn_finish(num1, den1, num2, den2, lam, g, lam_init):
    o = num1 / den1 - lam * (num2 / den2)
    ms = jnp.mean(o * o, axis=-1, keepdims=True)
    return o * lax.rsqrt(ms + RMS_EPS) * (g * (1.0 - lam_init))


def _chunk_mask(shape, tq):
    row = lax.broadcasted_iota(jnp.int32, shape, 0)
    col = lax.broadcasted_iota(jnp.int32, shape, 1)
    qc = jnp.where(row >= tq, row - tq, row) // CHUNK_MASK
    return col // CHUNK_MASK <= qc


def _attn_prompt_kernel(q_ref, k_ref, v_ref, lq1, lk1, lq2, lk2, g_ref, o_ref, acc_ref, *, tq, tk, lam_init):
    i = pl.program_id(2)
    qs = _stack_halves(q_ref[...])
    ones = jnp.ones((tk, HEAD_A), BF16)
    nfull = (i * tq) // tk

    def tile(j, first_masked_chunk):
        rows = pl.ds(pl.multiple_of(j * tk, tk), tk)
        p = jnp.exp2(_dg(qs, k_ref[rows, :], NT))
        if first_masked_chunk is not None:
            row = lax.broadcasted_iota(jnp.int32, p.shape, 0)
            col = lax.broadcasted_iota(jnp.int32, p.shape, 1)
            qc = jnp.where(row >= tq, row - tq, row) // CHUNK_MASK + first_masked_chunk
            p = jnp.where(col // CHUNK_MASK <= qc, p, 0.0)
        ve = jnp.concatenate([v_ref[rows, :], ones], axis=1)
        return _dg(p.astype(BF16), ve)

    acc_ref[...] = tile(nfull, (i * tq - nfull * tk) // CHUNK_MASK)

    def body(j, c):
        acc_ref[...] += tile(j, None)
        return c

    lax.fori_loop(0, nfull, body, 0)
    acc = acc_ref[...]
    lam = _lambda(lq1[...], lk1[...], lq2[...], lk2[...], lam_init)
    o = _attn_finish(acc[:tq, :HEAD_A], acc[:tq, HEAD_A:], acc[tq:, :HEAD_A], acc[tq:, HEAD_A:],
                     lam, g_ref[...], lam_init)
    o_ref[...] = o.astype(o_ref.dtype)


def _attn_prompt_safe_kernel(q_ref, k_ref, v_ref, lq1, lk1, lq2, lk2, g_ref, o_ref, *, tq, lam_init):
    i = pl.program_id(2)
    qs = _stack_halves(q_ref[...])

    def tile(j, carry, masked):
        m, l, acc = carry
        start = pl.multiple_of(j * tq, tq)
        k = k_ref[pl.ds(start, tq), :]
        v = v_ref[pl.ds(start, tq), :]
        s = _dg(qs, k, NT)
        if masked:
            s = jnp.where(_chunk_mask(s.shape, tq), s, NEG_BIG)
        m_new = jnp.maximum(m, jnp.max(s, axis=-1, keepdims=True))
        p = jnp.exp2(s - m_new)
        alpha = jnp.exp2(m - m_new)
        l = alpha * l + jnp.sum(p, axis=-1, keepdims=True)
        acc = alpha * acc + _dg(p.astype(BF16), v)
        return m_new, l, acc

    init = (jnp.full((2 * tq, 1), NEG_BIG, F32), jnp.zeros((2 * tq, 1), F32),
            jnp.zeros((2 * tq, HEAD_A), F32))
    carry = lax.fori_loop(0, i, lambda j, c: tile(j, c, False), init)
    _, l, acc = tile(i, carry, True)
    lam = _lambda(lq1[...], lk1[...], lq2[...], lk2[...], lam_init)
    o = _attn_finish(acc[:tq], l[:tq], acc[tq:], l[tq:], lam, g_ref[...], lam_init)
    o_ref[...] = o.astype(o_ref.dtype)


def _attn_prompt(q, k, v, lams, g, lam_init, logit_bound, *, tq, tk):
    b, t, width = q.shape
    nh = width // HEAD_A
    small = pl.BlockSpec((1, HALF_A), lambda bi, h, i: (0, 0))

    def call(body, scratch, name, **kw):
        return pl.pallas_call(
            functools.partial(body, tq=tq, lam_init=lam_init, **kw),
            grid=(b, nh, t // tq),
            in_specs=[
                pl.BlockSpec((None, tq, HEAD_A), lambda bi, h, i: (bi, i, h)),
                pl.BlockSpec((None, t, HEAD_A), lambda bi, h, i: (bi, 0, h)),
                pl.BlockSpec((None, t, HEAD_A), lambda bi, h, i: (bi, 0, h)),
                small, small, small, small,
                pl.BlockSpec((1, HEAD_A), lambda bi, h, i: (0, 0)),
            ],
            out_specs=pl.BlockSpec((None, tq, HEAD_A), lambda bi, h, i: (bi, i, h)),
            out_shape=jax.ShapeDtypeStruct((b, t, width), BF16),
            scratch_shapes=scratch,
            compiler_params=_params("parallel", "parallel", "arbitrary"),
            name=name,
        )

    fast = call(_attn_prompt_kernel, [pltpu.VMEM((2 * tq, 2 * HEAD_A), F32)], "attn_prompt", tk=tk)
    safe = call(_attn_prompt_safe_kernel, [], "attn_prompt_safe")
    return lax.cond(logit_bound <= MAX_UNSHIFTED_LOGIT2, fast, safe, q, k, v, *lams, g)


def _attn_sample_kernel(q_ref, kp_ref, vp_ref, kn_ref, vn_ref, lq1, lk1, lq2, lk2, g_ref, o_ref,
                        *, tq, past, nh, lam_init):
    lam = _lambda(lq1[...], lk1[...], lq2[...], lk2[...], lam_init)
    for h in range(nh):
        cols = slice(h * HEAD_A, (h + 1) * HEAD_A)
        head_rows = pl.ds(h, past, stride=nh)
        qs = _stack_halves(q_ref[:, cols])
        s_p = _dg(qs, kp_ref[head_rows, :].astype(BF16), NT)
        s_n = _dg(qs, kn_ref[:, cols], NT)
        row = lax.broadcasted_iota(jnp.int32, s_n.shape, 0)
        col = lax.broadcasted_iota(jnp.int32, s_n.shape, 1)
        qc = (past + jnp.where(row >= tq, row - tq, row)) // CHUNK_MASK
        s_n = jnp.where((past + col) // CHUNK_MASK <= qc, s_n, NEG_BIG)
        m = jnp.maximum(jnp.max(s_p, axis=-1, keepdims=True), jnp.max(s_n, axis=-1, keepdims=True))
        p_p = jnp.exp2(s_p - m)
        p_n = jnp.exp2(s_n - m)
        l = jnp.sum(p_p, axis=-1, keepdims=True) + jnp.sum(p_n, axis=-1, keepdims=True)
        acc = (_dg(p_p.astype(BF16), vp_ref[head_rows, :].astype(BF16))
               + _dg(p_n.astype(BF16), vn_ref[:, cols]))
        o = _attn_finish(acc[:tq], l[:tq], acc[tq:], l[tq:], lam, g_ref[...], lam_init)
        o_ref[:, cols] = o.astype(o_ref.dtype)


def _attn_sample(q, k_past, v_past, layer, k_new, v_new, lams, g, lam_init):
    b, t, width = q.shape
    depth, _, past, nh, _ = k_past.shape
    k_past = k_past.reshape(depth, b, past * nh, HEAD_A)
    v_past = v_past.reshape(depth, b, past * nh, HEAD_A)
    small = pl.BlockSpec((1, HALF_A), lambda bi: (0, 0))
    new = pl.BlockSpec((None, t, width), lambda bi: (bi, 0, 0))
    old = pl.BlockSpec((None, None, past * nh, HEAD_A), lambda bi: (layer, bi, 0, 0))
    return pl.pallas_call(
        functools.partial(_attn_sample_kernel, tq=t, past=past, nh=nh, lam_init=lam_init),
        grid=(b,),
        in_specs=[new, old, old, new, new, small, small, small, small,
                  pl.BlockSpec((1, HEAD_A), lambda bi: (0, 0))],
        out_specs=new,
        out_shape=jax.ShapeDtypeStruct((b, t, width), BF16),
        compiler_params=_params("parallel"),
        name="attn_sample",
    )(q, k_past, v_past, k_new, v_new, *lams, g)


def _stack_heads(x):
    lane = lax.broadcasted_iota(jnp.int32, x.shape, 1)
    return jnp.concatenate([jnp.where(lane < HEAD_B, x, 0.0), jnp.where(lane < HEAD_B, 0.0, x)], axis=0)


def _rwkv_kernel(pr_ref, pk_ref, pv_ref, pl_ref, sr_ref, sk_ref, sv_ref, sl_ref,
                 mr_ref, mk_ref, mv_ref, ml_ref,
                 w0_ref, a0_ref, kk_ref, ka_ref, rk_ref, lg_ref, lb_ref,
                 w2_ref, a2_ref, g2_ref, s0_ref,
                 out_ref, sout_ref,
                 s_sc, cr_sc, ck_sc, cv_sc, cl_sc, *, tb, ng):
    t = pl.program_id(2)
    width = ng * LANES

    @pl.when(t == 0)
    def _():
        s_sc[...] = s0_ref[...]
        cr_sc[...] = sr_ref[...]
        ck_sc[...] = sk_ref[...]
        cv_sc[...] = sv_ref[...]
        cl_sc[...] = sl_ref[...]

    def shifted(p_ref, carry_ref, mu_ref):
        p = p_ref[...]
        prev = pltpu.roll(p, 1, axis=0)
        row = lax.broadcasted_iota(jnp.int32, p.shape, 0)
        prev = jnp.where(row == 0, carry_ref[...], prev)
        carry_ref[...] = p[tb - 1:tb, :]
        return p + (prev - p) * mu_ref[...]

    xr = shifted(pr_ref, cr_sc, mr_ref)
    xk = shifted(pk_ref, ck_sc, mk_ref)
    xv = shifted(pv_ref, cv_sc, mv_ref)
    xl = shifted(pl_ref, cl_sc, ml_ref)

    seg = _group_matrix(width, HEAD_B, 1.0)
    z = w0_ref[...] + _mm3(jnp.tanh(xl), w2_ref[...])
    nz = -z
    softplus = jnp.maximum(nz, 0.0) + jnp.log(1.0 + jnp.exp(-jnp.abs(nz)))
    logw = -jnp.exp(-softplus - 0.5)
    ag = _sigmoid(a0_ref[...] + _mm3(xl, a2_ref[...]))
    gate = _mm3(_sigmoid(xl), g2_ref[...])
    kk = xk * kk_ref[...]
    kk = kk / jnp.maximum(jnp.sqrt(_mm2(kk * kk, seg)), 1e-12)
    km = xk * (1.0 + (ag - 1.0) * ka_ref[...])
    bonus = _mm2(xr * km * rk_ref[...], seg) * xv

    c = CHUNK_B
    c2 = 2 * c
    ri = lax.broadcasted_iota(jnp.int32, (c, c), 0)
    ci = lax.broadcasted_iota(jnp.int32, (c, c), 1)
    tri = jnp.where(ci <= ri, 1.0, 0.0).astype(BF16)
    r2 = lax.broadcasted_iota(jnp.int32, (c2, c2), 0)
    c2i = lax.broadcasted_iota(jnp.int32, (c2, c2), 1)
    rm = r2 % c
    cm = c2i % c
    strict = cm < rm
    incl = cm <= rm
    eye = jnp.where(r2 == c2i, 1.0, 0.0)

    def off_blocks(s):
        rs = rm // s
        return (rs - cm // s) * 2 + rs % 2 == 3

    nchunk = tb // c
    items = [(g, n) for n in range(nchunk) for g in range(ng)]

    pre = {}
    for g, n in items:
        rows = slice(n * c, (n + 1) * c)
        cols = slice(g * LANES, (g + 1) * LANES)
        lw = logw[rows, cols]
        h1 = lw.astype(BF16)
        r1 = lw - h1.astype(F32)
        h2 = r1.astype(BF16)
        h3 = (r1 - h2.astype(F32)).astype(BF16)
        cum = _dg(tri, h1) + (_dg(tri, h2) + _dg(tri, h3))
        w_in = jnp.exp(cum)
        w_ex = jnp.exp(cum - lw)
        w_inv = jnp.exp(-cum)
        kc = kk[rows, cols]
        a_s = _stack_heads(-kc * w_ex)
        r_h = _stack_heads(xr[rows, cols] * w_in).astype(BF16)
        bk = jnp.concatenate([_stack_heads(kc * ag[rows, cols] * w_inv),
                              _stack_heads(km[rows, cols] * w_inv)], axis=0)
        v_s = _stack_heads(xv[rows, cols])
        pre[g, n] = dict(a_s=a_s, a_sp=_split(a_s), r_h=r_h, bk=_split(bk), v_s=v_s, v_sp=_split(v_s),
                         w_last=w_in[c - 1:c, :])

    for key in items:
        d = pre[key]
        g_a = _d3(*d["a_sp"], *d["bk"], NT)
        g_r = _dg(d["r_h"], d["bk"][0], NT)
        d["n_ab"] = jnp.where(strict, g_a[:, :c2], 0.0)
        n_ak = jnp.where(strict, g_a[:, c2:], 0.0)
        d["n_rb"] = jnp.where(incl, g_r[:, :c2], 0.0).astype(BF16)
        n_rk = jnp.where(incl, g_r[:, c2:], 0.0).astype(BF16)
        d["akv"] = _d3(*_split(n_ak), *d["v_sp"])
        d["yv"] = _dg(n_rk, d["v_sp"][0])
        d["vk"] = _d3(*_split(d["v_s"].T), d["bk"][0][c2:], d["bk"][1][c2:])

    for key in items:
        d = pre[key]
        d["tinv"] = eye + jnp.where(off_blocks(1), d["n_ab"], 0.0)
    s = 2
    while s < c:
        sel = off_blocks(s)
        for key in items:
            d = pre[key]
            d["tmp"] = _mm3(jnp.where(sel, d["n_ab"], 0.0), d["tinv"])
        for key in items:
            d = pre[key]
            d["tinv"] = d["tinv"] + _mm3(d["tinv"], d["tmp"])
        s *= 2

    for key in items:
        d = pre[key]
        tx = _mm3(d["tinv"], jnp.concatenate([d["a_s"], d["akv"]], axis=1))
        d["ta"] = _split(tx[:, :c2])
        d["u0"] = tx[:, c2:]
        d["u0t"] = tx[:, c2:].T

    state = [s_sc[g] for g in range(ng)]
    ys = {}
    for n in range(nchunk):
        for g in range(ng):
            d = pre[g, n]
            s_h, s_l = _split(state[g])
            ut = _d3(s_h, s_l, *d["ta"], NT) + d["u0t"]
            u1 = _dg(d["ta"][0], s_h, NT) + d["u0"]
            y_s = _dg(d["r_h"], s_h, NT) + _dg(d["n_rb"], u1.astype(BF16)) + d["yv"]
            ys[g, n] = y_s[:c] + y_s[c:]
            upd = _d3(*_split(ut), d["bk"][0][:c2], d["bk"][1][:c2])
            state[g] = (state[g] + upd + d["vk"]) * d["w_last"]
    for g in range(ng):
        s_sc[g] = state[g]

    @pl.when(t == pl.num_programs(2) - 1)
    def _():
        for g in range(ng):
            sout_ref[g] = state[g]

    y = jnp.concatenate(
        [jnp.concatenate([ys[g, n] for n in range(nchunk)], axis=0) for g in range(ng)], axis=1)
    segm = _group_matrix(width, HEAD_B, 1.0 / HEAD_B)
    mean = _mm2(y, segm)
    dlt = y - mean
    var = _mm2(dlt * dlt, segm)
    yn = dlt * lax.rsqrt(var + GN_EPS) * lg_ref[...] + lb_ref[...]
    out_ref[...] = ((yn + bonus) * gate).astype(out_ref.dtype)


def _rwkv(p, shift0, s0_bd, mu, vecs, loras, *, tb, ng):
    b, t, _ = p.shape
    width = vecs[0].shape[-1]
    wblk = ng * LANES
    nblk = width // wblk
    lcol = 3 * width // (2 * LANES)

    def col(off):
        return lambda bi, g, ti: (bi, ti, off + g)

    def row(off):
        return lambda bi, g, ti: (bi, 0, off + g)

    def par(off):
        return lambda bi, g, ti: (0, off + g)

    in_specs = (
        [pl.BlockSpec((None, tb, wblk), col(k * nblk)) for k in range(3)]
        + [pl.BlockSpec((None, tb, 2 * LANES), lambda bi, g, ti: (bi, ti, lcol))]
        + [pl.BlockSpec((None, 1, wblk), row(k * nblk)) for k in range(3)]
        + [pl.BlockSpec((None, 1, 2 * LANES), lambda bi, g, ti: (bi, 0, lcol))]
        + [pl.BlockSpec((1, wblk), par(k * nblk)) for k in range(3)]
        + [pl.BlockSpec((1, 2 * LANES), lambda bi, g, ti: (0, lcol))]
        + [pl.BlockSpec((1, wblk), par(0)) for _ in vecs]
        + [pl.BlockSpec((2 * LANES, wblk), par(0)) for _ in loras]
        + [pl.BlockSpec((None, ng, LANES, LANES), lambda bi, g, ti: (bi, g, 0, 0))]
    )
    return pl.pallas_call(
        functools.partial(_rwkv_kernel, tb=tb, ng=ng),
        grid=(b, nblk, t // tb),
        in_specs=in_specs,
        out_specs=[pl.BlockSpec((None, tb, wblk), col(0)),
                   pl.BlockSpec((None, ng, LANES, LANES), lambda bi, g, ti: (bi, g, 0, 0))],
        out_shape=[jax.ShapeDtypeStruct((b, t, width), BF16),
                   jax.ShapeDtypeStruct((b, width // LANES, LANES, LANES), F32)],
        scratch_shapes=[pltpu.VMEM((ng, LANES, LANES), F32), pltpu.VMEM((1, wblk), F32),
                        pltpu.VMEM((1, wblk), F32), pltpu.VMEM((1, wblk), F32),
                        pltpu.VMEM((1, 2 * LANES), F32)],
        compiler_params=_params("parallel", "parallel", "arbitrary"),
        name="rwkv",
    )(p, p, p, p, shift0, shift0, shift0, shift0, mu, mu, mu, mu, *vecs, *loras, s0_bd)


def _outproj_kernel(x_ref, a_ref, b_ref, wa_ref, wb_ref, o_ref):
    o_ref[...] = x_ref[...] + (_dg(a_ref[...], wa_ref[...]) + _dg(b_ref[...], wb_ref[...]))


def _outproj(x2d, a, b, w, *, tm):
    n, d = x2d.shape
    half = a.shape[1]
    return pl.pallas_call(
        _outproj_kernel,
        grid=(n // tm,),
        in_specs=[pl.BlockSpec((tm, d), lambda i: (i, 0)),
                  pl.BlockSpec((tm, half), lambda i: (i, 0)),
                  pl.BlockSpec((tm, half), lambda i: (i, 0)),
                  pl.BlockSpec((half, d), lambda i: (0, 0)),
                  pl.BlockSpec((half, d), lambda i: (1, 0))],
        out_specs=pl.BlockSpec((tm, d), lambda i: (i, 0)),
        out_shape=jax.ShapeDtypeStruct((n, d), F32),
        compiler_params=_params("parallel"),
        name="outproj",
    )(x2d, a, b, w, w)


def _ffn_kernel(h_ref, g_ref, wg_ref, wu_ref, wd_ref, o_ref, hn_ref):
    @pl.when(pl.program_id(1) == 0)
    def _():
        h = h_ref[...]
        ms = jnp.mean(h * h, axis=-1, keepdims=True)
        hn_ref[...] = (h * lax.rsqrt(ms + RMS_EPS) * g_ref[...]).astype(BF16)
        o_ref[...] = h

    hn = hn_ref[...]
    gt = _dg(hn, wg_ref[...])
    up = _dg(hn, wu_ref[...])
    act = gt * _sigmoid(gt) * up
    o_ref[...] += _dg(act.astype(BF16), wd_ref[...])


def _ffn(h2d, g, wg, wu, wd, *, tm, tf):
    n, d = h2d.shape
    dff = wg.shape[1]
    return pl.pallas_call(
        _ffn_kernel,
        grid=(n // tm, dff // tf),
        in_specs=[pl.BlockSpec((tm, d), lambda i, j: (i, 0)),
                  pl.BlockSpec((1, d), lambda i, j: (0, 0)),
                  pl.BlockSpec((d, tf), lambda i, j: (0, j)),
                  pl.BlockSpec((d, tf), lambda i, j: (0, j)),
                  pl.BlockSpec((tf, d), lambda i, j: (j, 0))],
        out_specs=pl.BlockSpec((tm, d), lambda i, j: (i, 0)),
        out_shape=jax.ShapeDtypeStruct((n, d), F32),
        scratch_shapes=[pltpu.VMEM((tm, d), BF16)],
        compiler_params=_params("parallel", "arbitrary"),
        name="ffn",
    )(h2d, g.reshape(1, d), wg, wu, wd)


def _pick(n, pref):
    return pref if n % pref == 0 else n


def _block_diag_state(s):
    b, h = s.shape[:2]
    s = s.reshape(b, h // 2, 2, HEAD_B, HEAD_B)
    z = jnp.zeros_like(s[:, :, 0])
    top = jnp.concatenate([s[:, :, 0], z], axis=-1)
    bot = jnp.concatenate([z, s[:, :, 1]], axis=-1)
    return jnp.concatenate([top, bot], axis=-2)


def _unblock_state(s_bd):
    b, g = s_bd.shape[:2]
    s = jnp.stack([s_bd[:, :, :HEAD_B, :HEAD_B], s_bd[:, :, HEAD_B:, HEAD_B:]], axis=2)
    return s.reshape(b, 2 * g, HEAD_B, HEAD_B)


def _layer(x, pos, past, s0, shift0, lam_init, wts):
    b, t, d = x.shape
    n = b * t
    wa = wts["width_a"]
    wb = wts["width_b"]
    x2d = x.reshape(n, d)
    tm = _pick(n, 512)
    tm_in = _pick(n, 1024)
    rope = _rope_tables(pos, max(t, tm_in))
    lams = wts["lams"]

    q, k32, k16, v32, v16, p_rwkv = _inproj(
        x2d, wts["norm1_g"], wts["w_in"], wts["q_norm_g"], wts["k_norm_g"], rope,
        width_a=wa, q_scale=HALF_A ** -0.5 * LOG2E, tm=tm_in, tn=256)

    q = q.reshape(b, t, wa)
    k16 = k16.reshape(b, t, wa)
    v16 = v16.reshape(b, t, wa)
    if past is None:
        bound = (8.0 * 1.01 * LOG2E) * jnp.max(jnp.abs(wts["q_norm_g"])) * jnp.max(jnp.abs(wts["k_norm_g"]))
        attn = _attn_prompt(q, k16, v16, lams, wts["subln_g"], lam_init, bound,
                            tq=_pick(t, 512), tk=_pick(t, 1024))
        tb, ng = _pick(t, 256), 2
    else:
        attn = _attn_sample(q, *past, k16, v16, lams, wts["subln_g"], lam_init)
        tb, ng = _pick(t, CHUNK_B), wb // LANES

    p_rwkv = p_rwkv.reshape(b, t, -1)
    rw, s_bd = _rwkv(p_rwkv, shift0, s0, wts["mu"], wts["vecs"], wts["loras"], tb=tb, ng=ng)

    h = _outproj(x2d, attn.reshape(n, wa), rw.reshape(n, wb), wts["w_out"], tm=tm)
    y = _ffn(h, wts["norm2_g"], wts["w_gate"], wts["w_up"], wts["w_down"], tm=tm, tf=512)

    nh_a = wa // HEAD_A
    rcols = wts["rwkv_cols"]
    return (y.reshape(b, t, d), k32.reshape(b, t, nh_a, HEAD_A), v32.reshape(b, t, nh_a, HEAD_A),
            _unblock_state(s_bd), p_rwkv[:, t - 1:, :rcols])


def kernel(x_prompt, x_sample, cache_attn_k, cache_attn_v, state_rwkv, state_rwkv_shift, norm1_g, w_in, q_norm_g, k_norm_g, lambda_q1, lambda_k1, lambda_q2, lambda_k2, subln_g, mu_rwkv, w0, w2, a0, a2, g2, k_k, k_a, r_k, lnx_g, lnx_b, w_out, norm2_g, w_gate, w_up, w_down):
    depth = w_in.shape[0]
    bp, tp, d = x_prompt.shape
    bs, ts, _ = x_sample.shape
    past_len = cache_attn_k.shape[2]
    nh_a = cache_attn_k.shape[3]
    wa = nh_a * HEAD_A
    wb = d - wa
    rcols = 3 * wb + 3 * LORA
    pad = 2 * LANES - 3 * LORA
    pos_p = jnp.arange(tp, dtype=jnp.int32)
    pos_s = past_len + jnp.arange(ts, dtype=jnp.int32)

    xp, xs = x_prompt, x_sample
    outs = [[] for _ in range(8)]
    for l in range(depth):
        lam_init = 0.8 - 0.6 * math.exp(-0.3 * l)
        wl = w_in[l].astype(BF16)

        def lora_pad(w, slot):
            return jnp.pad(w, ((slot * LORA, 2 * LANES - (slot + 1) * LORA), (0, 0)))

        row = lambda v: v.reshape(1, -1)
        wts = dict(
            width_a=wa, width_b=wb, rwkv_cols=rcols,
            norm1_g=norm1_g[l], norm2_g=norm2_g[l],
            w_in=jnp.pad(wl, ((0, 0), (0, pad))),
            q_norm_g=row(q_norm_g[l]), k_norm_g=row(k_norm_g[l]),
            lams=[row(lambda_q1[l]), row(lambda_k1[l]), row(lambda_q2[l]), row(lambda_k2[l])],
            subln_g=row(subln_g[l]),
            mu=jnp.pad(row(mu_rwkv[l]), ((0, 0), (0, pad))),
            vecs=[row(w0[l]), row(a0[l]), row(k_k[l]), row(k_a[l]), row(r_k[l]), row(lnx_g[l]), row(lnx_b[l])],
            loras=[lora_pad(w2[l], 0), lora_pad(a2[l], 1), lora_pad(g2[l], 2)],
            w_out=w_out[l].astype(BF16), w_gate=w_gate[l].astype(BF16),
            w_up=w_up[l].astype(BF16), w_down=w_down[l].astype(BF16),
        )
        s0_p = jnp.zeros((bp, wb // LANES, LANES, LANES), F32)
        shift0_p = jnp.zeros((bp, 1, rcols + pad), F32)
        xp, kp, vp, sp, shp = _layer(xp, pos_p, None, s0_p, shift0_p, lam_init, wts)
        past = (cache_attn_k, cache_attn_v, l)
        shift0_s = jnp.pad(state_rwkv_shift[l], ((0, 0), (0, 0), (0, pad)))
        xs, ks, vs, ss, shs = _layer(xs, pos_s, past, _block_diag_state(state_rwkv[l]), shift0_s,
                                     lam_init, wts)
        for lst, val in zip(outs, (kp, vp, sp, shp, ks, vs, ss, shs)):
            lst.append(val)
    return (xp, xs) + tuple(jnp.stack(o) for o in outs)
```

```python
import functools
import math

import jax
import jax.numpy as jnp
from jax import lax
from jax.experimental import pallas as pl
from jax.experimental.pallas import tpu as pltpu

F32 = jnp.float32
BF16 = jnp.bfloat16

LANES = 128
HEAD_A = 128
HALF_A = 64
ROT_HALF = 8
ROPE_THETA = 500000.0
HEAD_B = 64
CHUNK_MASK = 64
CHUNK_B = 64
LORA = 64
RMS_EPS = 1e-6
GN_EPS = 64e-5
NEG_BIG = -1e30
LOG2E = 1.4426950408889634
MAX_UNSHIFTED_LOGIT2 = 64.0
VMEM_LIMIT = 56 * 1024 * 1024

NN = ((1,), (0,))
NT = ((1,), (1,))


def _dg(a, b, dims=NN):
    return lax.dot_general(a, b, (dims, ((), ())), preferred_element_type=F32)


def _split(a):
    hi = a.astype(BF16)
    lo = (a - hi.astype(F32)).astype(BF16)
    return hi, lo


def _d3(ah, al, bh, bl, dims=NN):
    return _dg(ah, bh, dims) + (_dg(ah, bl, dims) + _dg(al, bh, dims))


def _mm3(a, b, dims=NN):
    return _d3(*_split(a), *_split(b), dims)


def _mm2(a, b_exact):
    ah, al = _split(a)
    return _dg(ah, b_exact) + _dg(al, b_exact)


def _group_matrix(n, group, value):
    r = lax.broadcasted_iota(jnp.int32, (n, n), 0) // group
    c = lax.broadcasted_iota(jnp.int32, (n, n), 1) // group
    return jnp.where(r == c, value, 0.0).astype(BF16)


def _sigmoid(x):
    return 1.0 / (1.0 + jnp.exp(-x))


def _params(*sem):
    return pltpu.CompilerParams(dimension_semantics=sem, vmem_limit_bytes=VMEM_LIMIT)


def _inproj_kernel(x_ref, g_ref, w_ref, qg_ref, kg_ref, cos_ref, sa_ref, sb_ref,
                   q_ref, k32_ref, k16_ref, v32_ref, v16_ref, p_ref, xn_ref, *, nsec, q_scale):
    j = pl.program_id(1)

    @pl.when(j == 0)
    def _():
        x = x_ref[...]
        ms = jnp.mean(x * x, axis=-1, keepdims=True)
        xn_ref[...] = (x * lax.rsqrt(ms + RMS_EPS) * g_ref[...]).astype(BF16)

    acc = _dg(xn_ref[...], w_ref[...])

    @pl.when(j < 2 * nsec)
    def _():
        is_q = j < nsec
        seg = _group_matrix(LANES, HALF_A, 1.0 / HALF_A)
        hg = jnp.where(is_q, qg_ref[...] * q_scale, kg_ref[...])
        cos, sa, sb = cos_ref[...], sa_ref[...], sb_ref[...]
        groups = []
        for g in range(acc.shape[1] // LANES):
            xg = acc[:, g * LANES:(g + 1) * LANES]
            ms = _mm2(xg * xg, seg)
            xg = xg * lax.rsqrt(ms + RMS_EPS) * hg
            up = pltpu.roll(xg, LANES - ROT_HALF, axis=1)
            dn = pltpu.roll(xg, ROT_HALF, axis=1)
            groups.append(xg * cos + up * sa + dn * sb)
        rot = jnp.concatenate(groups, axis=1)

        @pl.when(is_q)
        def _():
            q_ref[...] = rot.astype(BF16)

        @pl.when(jnp.logical_not(is_q))
        def _():
            k32_ref[...] = rot
            k16_ref[...] = rot.astype(BF16)

    @pl.when(jnp.logical_and(j >= 2 * nsec, j < 3 * nsec))
    def _():
        v32_ref[...] = acc
        v16_ref[...] = acc.astype(BF16)

    @pl.when(j >= 3 * nsec)
    def _():
        p_ref[...] = acc


def _inproj(x2d, g, w, qg, kg, rope, *, width_a, q_scale, tm, tn):
    n, d = x2d.shape
    ncol = w.shape[1]
    nsec = width_a // tn
    nblk = rope[0].shape[0] // tm
    sec = lambda k: (lambda i, j: (i, jnp.clip(j - k * nsec, 0, nsec - 1)))
    vec = pl.BlockSpec((1, LANES), lambda i, j: (0, 0))
    tab = pl.BlockSpec((tm, LANES), lambda i, j: (i % nblk, 0))
    blk = lambda imap: pl.BlockSpec((tm, tn), imap)
    return pl.pallas_call(
        functools.partial(_inproj_kernel, nsec=nsec, q_scale=q_scale),
        grid=(n // tm, ncol // tn),
        in_specs=[pl.BlockSpec((tm, d), lambda i, j: (i, 0)),
                  pl.BlockSpec((1, d), lambda i, j: (0, 0)),
                  pl.BlockSpec((d, tn), lambda i, j: (0, j)),
                  vec, vec, tab, tab, tab],
        out_specs=[blk(sec(0)), blk(sec(1)), blk(sec(1)), blk(sec(2)), blk(sec(2)),
                   blk(lambda i, j: (i, jnp.maximum(j - 3 * nsec, 0)))],
        out_shape=[jax.ShapeDtypeStruct((n, width_a), BF16),
                   jax.ShapeDtypeStruct((n, width_a), F32), jax.ShapeDtypeStruct((n, width_a), BF16),
                   jax.ShapeDtypeStruct((n, width_a), F32), jax.ShapeDtypeStruct((n, width_a), BF16),
                   jax.ShapeDtypeStruct((n, ncol - 3 * width_a), F32)],
        scratch_shapes=[pltpu.VMEM((tm, d), BF16)],
        compiler_params=_params("parallel", "arbitrary"),
        name="inproj",
    )(x2d, g.reshape(1, d), w, qg, kg, *rope)


def _rope_tables(pos, rows):
    inv = jnp.float32(ROPE_THETA) ** (-jnp.arange(0, 2 * ROT_HALF, 2, dtype=F32) / (2 * ROT_HALF))
    ang = pos.astype(F32)[:, None] * inv
    cos8, sin8 = jnp.cos(ang), jnp.sin(ang)
    t = pos.shape[0]
    one = jnp.ones((t, HALF_A - 2 * ROT_HALF), F32)
    zero8 = jnp.zeros((t, ROT_HALF), F32)
    zrest = jnp.zeros((t, HALF_A - 2 * ROT_HALF), F32)
    cos_h = jnp.concatenate([cos8, cos8, one], axis=1)
    sa_h = jnp.concatenate([-sin8, zero8, zrest], axis=1)
    sb_h = jnp.concatenate([zero8, sin8, zrest], axis=1)
    tabs = [jnp.concatenate([h, h], axis=1) for h in (cos_h, sa_h, sb_h)]
    if rows > t:
        tabs = [jnp.tile(h, (rows // t, 1)) for h in tabs]
    return tabs


def _stack_halves(q):
    lane = lax.broadcasted_iota(jnp.int32, q.shape, 1)
    zero = jnp.zeros_like(q)
    return jnp.concatenate([jnp.where(lane < HALF_A, q, zero), jnp.where(lane < HALF_A, zero, q)], axis=0)


def _lambda(lq1, lk1, lq2, lk2, lam_init):
    return (jnp.exp(jnp.sum(lq1 * lk1, axis=-1, keepdims=True))
            - jnp.exp(jnp.sum(lq2 * lk2, axis=-1, keepdims=True)) + lam_init)


def _attn_finish(num1, den1, num2, den2, lam, g, lam_init):
    o = num1 / den1 - lam * (num2 / den2)
    ms = jnp.mean(o * o, axis=-1, keepdims=True)
    return o * lax.rsqrt(ms + RMS_EPS) * (g * (1.0 - lam_init))


def _chunk_mask(shape, tq):
    row = lax.broadcasted_iota(jnp.int32, shape, 0)
    col = lax.broadcasted_iota(jnp.int32, shape, 1)
    qc = jnp.where(row >= tq, row - tq, row) // CHUNK_MASK
    return col // CHUNK_MASK <= qc


def _attn_prompt_kernel(q_ref, k_ref, v_ref, lq1, lk1, lq2, lk2, g_ref, o_ref, acc_ref, *, tq, tk, lam_init):
    i = pl.program_id(2)
    qs = _stack_halves(q_ref[...])
    nfull = (i * tq) // tk
    nrest = (i * tq - nfull * tk) // tq

    def tile(start, width, masked):
        rows = pl.ds(pl.multiple_of(start, tq), width)
        p = jnp.exp2(_dg(qs, k_ref[rows, :], NT))
        if masked:
            p = jnp.where(_chunk_mask(p.shape, tq), p, 0.0)
        ve = jnp.concatenate([v_ref[rows, :], jnp.ones((width, HEAD_A), BF16)], axis=1)
        return _dg(p.astype(BF16), ve)

    acc_ref[...] = tile(i * tq, tq, True)

    def rest(j, c):
        acc_ref[...] += tile(nfull * tk + j * tq, tq, False)
        return c

    def full(j, c):
        acc_ref[...] += tile(j * tk, tk, False)
        return c

    lax.fori_loop(0, nrest, rest, 0)
    lax.fori_loop(0, nfull, full, 0)
    acc = acc_ref[...]
    lam = _lambda(lq1[...], lk1[...], lq2[...], lk2[...], lam_init)
    o = _attn_finish(acc[:tq, :HEAD_A], acc[:tq, HEAD_A:], acc[tq:, :HEAD_A], acc[tq:, HEAD_A:],
                     lam, g_ref[...], lam_init)
    o_ref[...] = o.astype(o_ref.dtype)


def _attn_prompt_safe_kernel(q_ref, k_ref, v_ref, lq1, lk1, lq2, lk2, g_ref, o_ref, *, tq, lam_init):
    i = pl.program_id(2)
    qs = _stack_halves(q_ref[...])

    def tile(j, carry, masked):
        m, l, acc = carry
        start = pl.multiple_of(j * tq, tq)
        k = k_ref[pl.ds(start, tq), :]
        v = v_ref[pl.ds(start, tq), :]
        s = _dg(qs, k, NT)
        if masked:
            s = jnp.where(_chunk_mask(s.shape, tq), s, NEG_BIG)
        m_new = jnp.maximum(m, jnp.max(s, axis=-1, keepdims=True))
        p = jnp.exp2(s - m_new)
        alpha = jnp.exp2(m - m_new)
        l = alpha * l + jnp.sum(p, axis=-1, keepdims=True)
        acc = alpha * acc + _dg(p.astype(BF16), v)
        return m_new, l, acc

    init = (jnp.full((2 * tq, 1), NEG_BIG, F32), jnp.zeros((2 * tq, 1), F32),
            jnp.zeros((2 * tq, HEAD_A), F32))
    carry = lax.fori_loop(0, i, lambda j, c: tile(j, c, False), init)
    _, l, acc = tile(i, carry, True)
    lam = _lambda(lq1[...], lk1[...], lq2[...], lk2[...], lam_init)
    o = _attn_finish(acc[:tq], l[:tq], acc[tq:], l[tq:], lam, g_ref[...], lam_init)
    o_ref[...] = o.astype(o_ref.dtype)


def _attn_prompt(q, k, v, lams, g, lam_init, logit_bound, *, tq, tk):
    b, t, width = q.shape
    nh = width // HEAD_A
    small = pl.BlockSpec((1, HALF_A), lambda bi, h, i: (0, 0))

    def call(body, scratch, name, **kw):
        return pl.pallas_call(
            functools.partial(body, tq=tq, lam_init=lam_init, **kw),
            grid=(b, nh, t // tq),
            in_specs=[
                pl.BlockSpec((None, tq, HEAD_A), lambda bi, h, i: (bi, i, h)),
                pl.BlockSpec((None, t, HEAD_A), lambda bi, h, i: (bi, 0, h)),
                pl.BlockSpec((None, t, HEAD_A), lambda bi, h, i: (bi, 0, h)),
                small, small, small, small,
                pl.BlockSpec((1, HEAD_A), lambda bi, h, i: (0, 0)),
            ],
            out_specs=pl.BlockSpec((None, tq, HEAD_A), lambda bi, h, i: (bi, i, h)),
            out_shape=jax.ShapeDtypeStruct((b, t, width), BF16),
            scratch_shapes=scratch,
            compiler_params=_params("parallel", "parallel", "arbitrary"),
            name=name,
        )

    fast = call(_attn_prompt_kernel, [pltpu.VMEM((2 * tq, 2 * HEAD_A), F32)], "attn_prompt", tk=tk)
    safe = call(_attn_prompt_safe_kernel, [], "attn_prompt_safe")
    return lax.cond(logit_bound <= MAX_UNSHIFTED_LOGIT2, fast, safe, q, k, v, *lams, g)


def _attn_sample_kernel(q_ref, kp_ref, vp_ref, kn_ref, vn_ref, lq1, lk1, lq2, lk2, g_ref, o_ref,
                        *, tq, past, nh, lam_init):
    lam = _lambda(lq1[...], lk1[...], lq2[...], lk2[...], lam_init)
    for h in range(nh):
        cols = slice(h * HEAD_A, (h + 1) * HEAD_A)
        head_rows = pl.ds(h, past, stride=nh)
        qs = _stack_halves(q_ref[:, cols])
        s_p = _dg(qs, kp_ref[head_rows, :].astype(BF16), NT)
        s_n = _dg(qs, kn_ref[:, cols], NT)
        row = lax.broadcasted_iota(jnp.int32, s_n.shape, 0)
        col = lax.broadcasted_iota(jnp.int32, s_n.shape, 1)
        qc = (past + jnp.where(row >= tq, row - tq, row)) // CHUNK_MASK
        s_n = jnp.where((past + col) // CHUNK_MASK <= qc, s_n, NEG_BIG)
        m = jnp.maximum(jnp.max(s_p, axis=-1, keepdims=True), jnp.max(s_n, axis=-1, keepdims=True))
        p_p = jnp.exp2(s_p - m)
        p_n = jnp.exp2(s_n - m)
        l = jnp.sum(p_p, axis=-1, keepdims=True) + jnp.sum(p_n, axis=-1, keepdims=True)
        acc = (_dg(p_p.astype(BF16), vp_ref[head_rows, :].astype(BF16))
               + _dg(p_n.astype(BF16), vn_ref[:, cols]))
        o = _attn_finish(acc[:tq], l[:tq], acc[tq:], l[tq:], lam, g_ref[...], lam_init)
        o_ref[:, cols] = o.astype(o_ref.dtype)


def _attn_sample(q, k_past, v_past, layer, k_new, v_new, lams, g, lam_init):
    b, t, width = q.shape
    depth, _, past, nh, _ = k_past.shape
    k_past = k_past.reshape(depth, b, past * nh, HEAD_A)
    v_past = v_past.reshape(depth, b, past * nh, HEAD_A)
    small = pl.BlockSpec((1, HALF_A), lambda bi: (0, 0))
    new = pl.BlockSpec((None, t, width), lambda bi: (bi, 0, 0))
    old = pl.BlockSpec((None, None, past * nh, HEAD_A), lambda bi: (layer, bi, 0, 0))
    return pl.pallas_call(
        functools.partial(_attn_sample_kernel, tq=t, past=past, nh=nh, lam_init=lam_init),
        grid=(b,),
        in_specs=[new, old, old, new, new, small, small, small, small,
                  pl.BlockSpec((1, HEAD_A), lambda bi: (0, 0))],
        out_specs=new,
        out_shape=jax.ShapeDtypeStruct((b, t, width), BF16),
        compiler_params=_params("parallel"),
        name="attn_sample",
    )(q, k_past, v_past, k_new, v_new, *lams, g)


def _stack_heads(x):
    lane = lax.broadcasted_iota(jnp.int32, x.shape, 1)
    return jnp.concatenate([jnp.where(lane < HEAD_B, x, 0.0), jnp.where(lane < HEAD_B, 0.0, x)], axis=0)


def _rwkv_kernel(pr_ref, pk_ref, pv_ref, pl_ref, sr_ref, sk_ref, sv_ref, sl_ref,
                 mr_ref, mk_ref, mv_ref, ml_ref,
                 w0_ref, a0_ref, kk_ref, ka_ref, rk_ref, lg_ref, lb_ref,
                 w2_ref, a2_ref, g2_ref, s0_ref,
                 out_ref, sout_ref,
                 s_sc, cr_sc, ck_sc, cv_sc, cl_sc, *, tb, ng):
    t = pl.program_id(2)
    width = ng * LANES

    @pl.when(t == 0)
    def _():
        s_sc[...] = s0_ref[...]
        cr_sc[...] = sr_ref[...]
        ck_sc[...] = sk_ref[...]
        cv_sc[...] = sv_ref[...]
        cl_sc[...] = sl_ref[...]

    def shifted(p_ref, carry_ref, mu_ref):
        p = p_ref[...]
        prev = pltpu.roll(p, 1, axis=0)
        row = lax.broadcasted_iota(jnp.int32, p.shape, 0)
        prev = jnp.where(row == 0, carry_ref[...], prev)
        carry_ref[...] = p[tb - 1:tb, :]
        return p + (prev - p) * mu_ref[...]

    xr = shifted(pr_ref, cr_sc, mr_ref)
    xk = shifted(pk_ref, ck_sc, mk_ref)
    xv = shifted(pv_ref, cv_sc, mv_ref)
    xl = shifted(pl_ref, cl_sc, ml_ref)

    seg = _group_matrix(width, HEAD_B, 1.0)
    z = w0_ref[...] + _mm3(jnp.tanh(xl), w2_ref[...])
    nz = -z
    softplus = jnp.maximum(nz, 0.0) + jnp.log(1.0 + jnp.exp(-jnp.abs(nz)))
    logw = -jnp.exp(-softplus - 0.5)
    ag = _sigmoid(a0_ref[...] + _mm3(xl, a2_ref[...]))
    gate = _mm3(_sigmoid(xl), g2_ref[...])
    kk = xk * kk_ref[...]
    kk = kk / jnp.maximum(jnp.sqrt(_mm2(kk * kk, seg)), 1e-12)
    km = xk * (1.0 + (ag - 1.0) * ka_ref[...])
    bonus = _mm2(xr * km * rk_ref[...], seg) * xv

    c = CHUNK_B
    c2 = 2 * c
    ri = lax.broadcasted_iota(jnp.int32, (c, c), 0)
    ci = lax.broadcasted_iota(jnp.int32, (c, c), 1)
    tri = jnp.where(ci <= ri, 1.0, 0.0).astype(BF16)
    r2 = lax.broadcasted_iota(jnp.int32, (c2, c2), 0)
    c2i = lax.broadcasted_iota(jnp.int32, (c2, c2), 1)
    rm = r2 % c
    cm = c2i % c
    strict = cm < rm
    incl = cm <= rm
    eye = jnp.where(r2 == c2i, 1.0, 0.0)

    def off_blocks(s):
        rs = rm // s
        return (rs - cm // s) * 2 + rs % 2 == 3

    nchunk = tb // c
    items = [(g, n) for n in range(nchunk) for g in range(ng)]

    pre = {}
    for g, n in items:
        rows = slice(n * c, (n + 1) * c)
        cols = slice(g * LANES, (g + 1) * LANES)
        lw = logw[rows, cols]
        h1 = lw.astype(BF16)
        r1 = lw - h1.astype(F32)
        h2 = r1.astype(BF16)
        h3 = (r1 - h2.astype(F32)).astype(BF16)
        cum = _dg(tri, h1) + (_dg(tri, h2) + _dg(tri, h3))
        w_in = jnp.exp(cum)
        w_ex = jnp.exp(cum - lw)
        w_inv = jnp.exp(-cum)
        kc = kk[rows, cols]
        ar = jnp.concatenate([_stack_heads(-kc * w_ex),
                              _stack_heads(xr[rows, cols] * w_in)], axis=0).astype(BF16)
        bk = jnp.concatenate([_stack_heads(kc * ag[rows, cols] * w_inv),
                              _stack_heads(km[rows, cols] * w_inv)], axis=0).astype(BF16)
        v_s = _stack_heads(xv[rows, cols])
        pre[g, n] = dict(ar=ar, bk=bk, v_h=v_s.astype(BF16), vt_h=v_s.T.astype(BF16), w_last=w_in[c - 1:c, :])

    for key in items:
        d = pre[key]
        gm = _dg(d["ar"], d["bk"], NT)
        d["n_ab"] = jnp.where(strict, gm[:c2, :c2], 0.0).astype(BF16)
        d["n_rb"] = jnp.where(incl, gm[c2:, :c2], 0.0).astype(BF16)
        n_k = jnp.concatenate([jnp.where(strict, gm[:c2, c2:], 0.0),
                               jnp.where(incl, gm[c2:, c2:], 0.0)], axis=0).astype(BF16)
        kv = _dg(n_k, d["v_h"])
        d["akv"] = kv[:c2].astype(BF16)
        d["yv"] = kv[c2:]
        d["vk"] = _dg(d["vt_h"], d["bk"][c2:])

    zero = jnp.zeros((c2, c2), BF16)
    for key in items:
        d = pre[key]
        d["tinv"] = eye + jnp.where(off_blocks(1), d["n_ab"], zero).astype(F32)
    s = 2
    while s < c:
        sel = off_blocks(s)
        for key in items:
            d = pre[key]
            d["tinv_h"] = d["tinv"].astype(BF16)
            d["tmp"] = _dg(jnp.where(sel, d["n_ab"], zero), d["tinv_h"]).astype(BF16)
        for key in items:
            d = pre[key]
            d["tinv"] = d["tinv"] + _dg(d["tinv_h"], d["tmp"])
        s *= 2

    for key in items:
        d = pre[key]
        tx = _dg(d["tinv"].astype(BF16), jnp.concatenate([d["ar"][:c2], d["akv"]], axis=1))
        d["tar"] = jnp.concatenate([tx[:, :c2].astype(BF16), d["ar"][c2:]], axis=0)
        d["u0"] = tx[:, c2:]
        d["u0t"] = tx[:, c2:].T

    state = [s_sc[g] for g in range(ng)]
    ys = {}
    for n in range(nchunk):
        for g in range(ng):
            d = pre[g, n]
            s_h = state[g].astype(BF16)
            ut = _dg(s_h, d["tar"][:c2], NT) + d["u0t"]
            us = _dg(d["tar"], s_h, NT)
            u1 = us[:c2] + d["u0"]
            y_s = us[c2:] + _dg(d["n_rb"], u1.astype(BF16)) + d["yv"]
            ys[g, n] = y_s[:c] + y_s[c:]
            state[g] = (state[g] + _dg(ut.astype(BF16), d["bk"][:c2]) + d["vk"]) * d["w_last"]
    for g in range(ng):
        s_sc[g] = state[g]

    @pl.when(t == pl.num_programs(2) - 1)
    def _():
        for g in range(ng):
            sout_ref[g] = state[g]

    y = jnp.concatenate(
        [jnp.concatenate([ys[g, n] for n in range(nchunk)], axis=0) for g in range(ng)], axis=1)
    segm = _group_matrix(width, HEAD_B, 1.0 / HEAD_B)
    mean = _mm2(y, segm)
    dlt = y - mean
    var = _mm2(dlt * dlt, segm)
    yn = dlt * lax.rsqrt(var + GN_EPS) * lg_ref[...] + lb_ref[...]
    out_ref[...] = ((yn + bonus) * gate).astype(out_ref.dtype)


def _rwkv(p, shift0, s0_bd, mu, vecs, loras, *, tb, ng):
    b, t, _ = p.shape
    width = vecs[0].shape[-1]
    wblk = ng * LANES
    nblk = width // wblk
    lcol = 3 * width // (2 * LANES)

    def col(off):
        return lambda bi, g, ti: (bi, ti, off + g)

    def row(off):
        return lambda bi, g, ti: (bi, 0, off + g)

    def par(off):
        return lambda bi, g, ti: (0, off + g)

    in_specs = (
        [pl.BlockSpec((None, tb, wblk), col(k * nblk)) for k in range(3)]
        + [pl.BlockSpec((None, tb, 2 * LANES), lambda bi, g, ti: (bi, ti, lcol))]
        + [pl.BlockSpec((None, 1, wblk), row(k * nblk)) for k in range(3)]
        + [pl.BlockSpec((None, 1, 2 * LANES), lambda bi, g, ti: (bi, 0, lcol))]
        + [pl.BlockSpec((1, wblk), par(k * nblk)) for k in range(3)]
        + [pl.BlockSpec((1, 2 * LANES), lambda bi, g, ti: (0, lcol))]
        + [pl.BlockSpec((1, wblk), par(0)) for _ in vecs]
        + [pl.BlockSpec((2 * LANES, wblk), par(0)) for _ in loras]
        + [pl.BlockSpec((None, ng, LANES, LANES), lambda bi, g, ti: (bi, g, 0, 0))]
    )
    return pl.pallas_call(
        functools.partial(_rwkv_kernel, tb=tb, ng=ng),
        grid=(b, nblk, t // tb),
        in_specs=in_specs,
        out_specs=[pl.BlockSpec((None, tb, wblk), col(0)),
                   pl.BlockSpec((None, ng, LANES, LANES), lambda bi, g, ti: (bi, g, 0, 0))],
        out_shape=[jax.ShapeDtypeStruct((b, t, width), BF16),
                   jax.ShapeDtypeStruct((b, width // LANES, LANES, LANES), F32)],
        scratch_shapes=[pltpu.VMEM((ng, LANES, LANES), F32), pltpu.VMEM((1, wblk), F32),
                        pltpu.VMEM((1, wblk), F32), pltpu.VMEM((1, wblk), F32),
                        pltpu.VMEM((1, 2 * LANES), F32)],
        compiler_params=_params("parallel", "parallel", "arbitrary"),
        name="rwkv",
    )(p, p, p, p, shift0, shift0, shift0, shift0, mu, mu, mu, mu, *vecs, *loras, s0_bd)


def _outproj_kernel(x_ref, a_ref, b_ref, wa_ref, wb_ref, o_ref):
    o_ref[...] = x_ref[...] + (_dg(a_ref[...], wa_ref[...]) + _dg(b_ref[...], wb_ref[...]))


def _outproj(x2d, a, b, w, *, tm):
    n, d = x2d.shape
    half = a.shape[1]
    return pl.pallas_call(
        _outproj_kernel,
        grid=(n // tm,),
        in_specs=[pl.BlockSpec((tm, d), lambda i: (i, 0)),
                  pl.BlockSpec((tm, half), lambda i: (i, 0)),
                  pl.BlockSpec((tm, half), lambda i: (i, 0)),
                  pl.BlockSpec((half, d), lambda i: (0, 0)),
                  pl.BlockSpec((half, d), lambda i: (1, 0))],
        out_specs=pl.BlockSpec((tm, d), lambda i: (i, 0)),
        out_shape=jax.ShapeDtypeStruct((n, d), F32),
        compiler_params=_params("parallel"),
        name="outproj",
    )(x2d, a, b, w, w)


def _ffn_kernel(h_ref, g_ref, wg_ref, wu_ref, wd_ref, o_ref, hn_ref):
    @pl.when(pl.program_id(1) == 0)
    def _():
        h = h_ref[...]
        ms = jnp.mean(h * h, axis=-1, keepdims=True)
        hn_ref[...] = (h * lax.rsqrt(ms + RMS_EPS) * g_ref[...]).astype(BF16)
        o_ref[...] = h

    hn = hn_ref[...]
    gt = _dg(hn, wg_ref[...])
    up = _dg(hn, wu_ref[...])
    act = gt * _sigmoid(gt) * up
    o_ref[...] += _dg(act.astype(BF16), wd_ref[...])


def _ffn(h2d, g, wg, wu, wd, *, tm, tf):
    n, d = h2d.shape
    dff = wg.shape[1]
    return pl.pallas_call(
        _ffn_kernel,
        grid=(n // tm, dff // tf),
        in_specs=[pl.BlockSpec((tm, d), lambda i, j: (i, 0)),
                  pl.BlockSpec((1, d), lambda i, j: (0, 0)),
                  pl.BlockSpec((d, tf), lambda i, j: (0, j)),
                  pl.BlockSpec((d, tf), lambda i, j: (0, j)),
                  pl.BlockSpec((tf, d), lambda i, j: (j, 0))],
        out_specs=pl.BlockSpec((tm, d), lambda i, j: (i, 0)),
        out_shape=jax.ShapeDtypeStruct((n, d), F32),
        scratch_shapes=[pltpu.VMEM((tm, d), BF16)],
        compiler_params=_params("parallel", "arbitrary"),
        name="ffn",
    )(h2d, g.reshape(1, d), wg, wu, wd)


def _pick(n, pref):
    return pref if n % pref == 0 else n


def _block_diag_state(s):
    b, h = s.shape[:2]
    s = s.reshape(b, h // 2, 2, HEAD_B, HEAD_B)
    z = jnp.zeros_like(s[:, :, 0])
    top = jnp.concatenate([s[:, :, 0], z], axis=-1)
    bot = jnp.concatenate([z, s[:, :, 1]], axis=-1)
    return jnp.concatenate([top, bot], axis=-2)


def _unblock_state(s_bd):
    b, g = s_bd.shape[:2]
    s = jnp.stack([s_bd[:, :, :HEAD_B, :HEAD_B], s_bd[:, :, HEAD_B:, HEAD_B:]], axis=2)
    return s.reshape(b, 2 * g, HEAD_B, HEAD_B)


def _layer(x, pos, past, s0, shift0, lam_init, wts):
    b, t, d = x.shape
    n = b * t
    wa = wts["width_a"]
    wb = wts["width_b"]
    x2d = x.reshape(n, d)
    tm = _pick(n, 512)
    tm_in = _pick(n, 1024)
    rope = _rope_tables(pos, max(t, tm_in))
    lams = wts["lams"]

    q, k32, k16, v32, v16, p_rwkv = _inproj(
        x2d, wts["norm1_g"], wts["w_in"], wts["q_norm_g"], wts["k_norm_g"], rope,
        width_a=wa, q_scale=HALF_A ** -0.5 * LOG2E, tm=tm_in, tn=256)

    q = q.reshape(b, t, wa)
    k16 = k16.reshape(b, t, wa)
    v16 = v16.reshape(b, t, wa)
    if past is None:
        bound = (8.0 * 1.01 * LOG2E) * jnp.max(jnp.abs(wts["q_norm_g"])) * jnp.max(jnp.abs(wts["k_norm_g"]))
        attn = _attn_prompt(q, k16, v16, lams, wts["subln_g"], lam_init, bound,
                            tq=_pick(t, 512), tk=_pick(t, 1024))
        tb, ng = _pick(t, 256), 4
    else:
        attn = _attn_sample(q, *past, k16, v16, lams, wts["subln_g"], lam_init)
        tb, ng = _pick(t, CHUNK_B), wb // LANES

    p_rwkv = p_rwkv.reshape(b, t, -1)
    rw, s_bd = _rwkv(p_rwkv, shift0, s0, wts["mu"], wts["vecs"], wts["loras"], tb=tb, ng=ng)

    h = _outproj(x2d, attn.reshape(n, wa), rw.reshape(n, wb), wts["w_out"], tm=tm)
    y = _ffn(h, wts["norm2_g"], wts["w_gate"], wts["w_up"], wts["w_down"], tm=tm, tf=512)

    nh_a = wa // HEAD_A
    rcols = wts["rwkv_cols"]
    return (y.reshape(b, t, d), k32.reshape(b, t, nh_a, HEAD_A), v32.reshape(b, t, nh_a, HEAD_A),
            _unblock_state(s_bd), p_rwkv[:, t - 1:, :rcols])


def kernel(x_prompt, x_sample, cache_attn_k, cache_attn_v, state_rwkv, state_rwkv_shift, norm1_g, w_in, q_norm_g, k_norm_g, lambda_q1, lambda_k1, lambda_q2, lambda_k2, subln_g, mu_rwkv, w0, w2, a0, a2, g2, k_k, k_a, r_k, lnx_g, lnx_b, w_out, norm2_g, w_gate, w_up, w_down):
    depth = w_in.shape[0]
    bp, tp, d = x_prompt.shape
    bs, ts, _ = x_sample.shape
    past_len = cache_attn_k.shape[2]
    nh_a = cache_attn_k.shape[3]
    wa = nh_a * HEAD_A
    wb = d - wa
    rcols = 3 * wb + 3 * LORA
    pad = 2 * LANES - 3 * LORA
    pos_p = jnp.arange(tp, dtype=jnp.int32)
    pos_s = past_len + jnp.arange(ts, dtype=jnp.int32)

    xp, xs = x_prompt, x_sample
    outs = [[] for _ in range(8)]
    for l in range(depth):
        lam_init = 0.8 - 0.6 * math.exp(-0.3 * l)
        wl = w_in[l].astype(BF16)

        def lora_pad(w, slot):
            return jnp.pad(w, ((slot * LORA, 2 * LANES - (slot + 1) * LORA), (0, 0)))

        row = lambda v: v.reshape(1, -1)
        wts = dict(
            width_a=wa, width_b=wb, rwkv_cols=rcols,
            norm1_g=norm1_g[l], norm2_g=norm2_g[l],
            w_in=jnp.pad(wl, ((0, 0), (0, pad))),
            q_norm_g=row(q_norm_g[l]), k_norm_g=row(k_norm_g[l]),
            lams=[row(lambda_q1[l]), row(lambda_k1[l]), row(lambda_q2[l]), row(lambda_k2[l])],
            subln_g=row(subln_g[l]),
            mu=jnp.pad(row(mu_rwkv[l]), ((0, 0), (0, pad))),
            vecs=[row(w0[l]), row(a0[l]), row(k_k[l]), row(k_a[l]), row(r_k[l]), row(lnx_g[l]), row(lnx_b[l])],
            loras=[lora_pad(w2[l], 0), lora_pad(a2[l], 1), lora_pad(g2[l], 2)],
            w_out=w_out[l].astype(BF16), w_gate=w_gate[l].astype(BF16),
            w_up=w_up[l].astype(BF16), w_down=w_down[l].astype(BF16),
        )
        s0_p = jnp.zeros((bp, wb // LANES, LANES, LANES), F32)
        shift0_p = jnp.zeros((bp, 1, rcols + pad), F32)
        xp, kp, vp, sp, shp = _layer(xp, pos_p, None, s0_p, shift0_p, lam_init, wts)
        past = (cache_attn_k, cache_attn_v, l)
        shift0_s = jnp.pad(state_rwkv_shift[l], ((0, 0), (0, 0), (0, pad)))
        xs, ks, vs, ss, shs = _layer(xs, pos_s, past, _block_diag_state(state_rwkv[l]), shift0_s,
                                     lam_init, wts)
        for lst, val in zip(outs, (kp, vp, sp, shp, ks, vs, ss, shs)):
            lst.append(val)
    return (xp, xs) + tuple(jnp.stack(o) for o in outs)
```

```python
import functools
import math

import jax
import jax.numpy as jnp
from jax import lax
from jax.experimental import pallas as pl
from jax.experimental.pallas import tpu as pltpu

F32 = jnp.float32
BF16 = jnp.bfloat16

LANES = 128
HEAD_A = 128
HALF_A = 64
ROT_HALF = 8
ROPE_THETA = 500000.0
HEAD_B = 64
CHUNK_MASK = 64
CHUNK_B = 64
LORA = 64
RMS_EPS = 1e-6
GN_EPS = 64e-5
NEG_BIG = -1e30
LOG2E = 1.4426950408889634
MAX_UNSHIFTED_LOGIT2 = 64.0
VMEM_LIMIT = 56 * 1024 * 1024
INPROJ_TN = 512

NN = ((1,), (0,))
NT = ((1,), (1,))


def _dg(a, b, dims=NN):
    return lax.dot_general(a, b, (dims, ((), ())), preferred_element_type=F32)


def _split(a):
    hi = a.astype(BF16)
    lo = (a - hi.astype(F32)).astype(BF16)
    return hi, lo


def _mm2(a, b_exact):
    ah, al = _split(a)
    return _dg(ah, b_exact) + _dg(al, b_exact)


def _group_matrix(n, group, value):
    r = lax.broadcasted_iota(jnp.int32, (n, n), 0) // group
    c = lax.broadcasted_iota(jnp.int32, (n, n), 1) // group
    return jnp.where(r == c, value, 0.0).astype(BF16)


def _sigmoid(x):
    return 1.0 / (1.0 + jnp.exp(-x))


def _params(*sem):
    return pltpu.CompilerParams(dimension_semantics=sem, vmem_limit_bytes=VMEM_LIMIT)


def _inproj_kernel(x_ref, g_ref, w_ref, qg_ref, kg_ref, cos_ref, sa_ref, sb_ref,
                   q_ref, k32_ref, k16_ref, v32_ref, v16_ref, p_ref, xn_ref, *, nsec, q_scale):
    j = pl.program_id(1)

    @pl.when(j == 0)
    def _():
        x = x_ref[...]
        ms = jnp.mean(x * x, axis=-1, keepdims=True)
        xn_ref[...] = (x * lax.rsqrt(ms + RMS_EPS) * g_ref[...]).astype(BF16)

    acc = _dg(xn_ref[...], w_ref[...])

    @pl.when(j < 2 * nsec)
    def _():
        is_q = j < nsec
        seg = _group_matrix(LANES, HALF_A, 1.0 / HALF_A)
        hg = jnp.where(is_q, qg_ref[...] * q_scale, kg_ref[...])
        cos, sa, sb = cos_ref[...], sa_ref[...], sb_ref[...]
        groups = []
        for g in range(acc.shape[1] // LANES):
            xg = acc[:, g * LANES:(g + 1) * LANES]
            ms = _mm2(xg * xg, seg)
            xg = xg * lax.rsqrt(ms + RMS_EPS) * hg
            up = pltpu.roll(xg, LANES - ROT_HALF, axis=1)
            dn = pltpu.roll(xg, ROT_HALF, axis=1)
            groups.append(xg * cos + up * sa + dn * sb)
        rot = jnp.concatenate(groups, axis=1)

        @pl.when(is_q)
        def _():
            q_ref[...] = rot.astype(BF16)

        @pl.when(jnp.logical_not(is_q))
        def _():
            k32_ref[...] = rot
            k16_ref[...] = rot.astype(BF16)

    @pl.when(jnp.logical_and(j >= 2 * nsec, j < 3 * nsec))
    def _():
        v32_ref[...] = acc
        v16_ref[...] = acc.astype(BF16)

    @pl.when(j >= 3 * nsec)
    def _():
        p_ref[...] = acc


def _inproj(x2d, g, w, qg, kg, rope, *, width_a, q_scale, tm, tn):
    n, d = x2d.shape
    ncol = w.shape[1]
    nsec = width_a // tn
    nblk = rope[0].shape[0] // tm
    sec = lambda k: (lambda i, j: (i, jnp.clip(j - k * nsec, 0, nsec - 1)))
    vec = pl.BlockSpec((1, LANES), lambda i, j: (0, 0))
    tab = pl.BlockSpec((tm, LANES), lambda i, j: (i % nblk, 0))
    blk = lambda imap: pl.BlockSpec((tm, tn), imap)
    return pl.pallas_call(
        functools.partial(_inproj_kernel, nsec=nsec, q_scale=q_scale),
        grid=(n // tm, ncol // tn),
        in_specs=[pl.BlockSpec((tm, d), lambda i, j: (i, 0)),
                  pl.BlockSpec((1, d), lambda i, j: (0, 0)),
                  pl.BlockSpec((d, tn), lambda i, j: (0, j)),
                  vec, vec, tab, tab, tab],
        out_specs=[blk(sec(0)), blk(sec(1)), blk(sec(1)), blk(sec(2)), blk(sec(2)),
                   blk(lambda i, j: (i, jnp.maximum(j - 3 * nsec, 0)))],
        out_shape=[jax.ShapeDtypeStruct((n, width_a), BF16),
                   jax.ShapeDtypeStruct((n, width_a), F32), jax.ShapeDtypeStruct((n, width_a), BF16),
                   jax.ShapeDtypeStruct((n, width_a), F32), jax.ShapeDtypeStruct((n, width_a), BF16),
                   jax.ShapeDtypeStruct((n, ncol - 3 * width_a), F32)],
        scratch_shapes=[pltpu.VMEM((tm, d), BF16)],
        compiler_params=_params("parallel", "arbitrary"),
        name="inproj",
    )(x2d, g.reshape(1, d), w, qg, kg, *rope)


def _rope_tables(pos, rows):
    inv = jnp.float32(ROPE_THETA) ** (-jnp.arange(0, 2 * ROT_HALF, 2, dtype=F32) / (2 * ROT_HALF))
    ang = pos.astype(F32)[:, None] * inv
    cos8, sin8 = jnp.cos(ang), jnp.sin(ang)
    t = pos.shape[0]
    one = jnp.ones((t, HALF_A - 2 * ROT_HALF), F32)
    zero8 = jnp.zeros((t, ROT_HALF), F32)
    zrest = jnp.zeros((t, HALF_A - 2 * ROT_HALF), F32)
    cos_h = jnp.concatenate([cos8, cos8, one], axis=1)
    sa_h = jnp.concatenate([-sin8, zero8, zrest], axis=1)
    sb_h = jnp.concatenate([zero8, sin8, zrest], axis=1)
    tabs = [jnp.concatenate([h, h], axis=1) for h in (cos_h, sa_h, sb_h)]
    if rows > t:
        tabs = [jnp.tile(h, (rows // t, 1)) for h in tabs]
    return tabs


def _stack_halves(q):
    lane = lax.broadcasted_iota(jnp.int32, q.shape, 1)
    zero = jnp.zeros_like(q)
    return jnp.concatenate([jnp.where(lane < HALF_A, q, zero), jnp.where(lane < HALF_A, zero, q)], axis=0)


def _lambda(lq1, lk1, lq2, lk2, lam_init):
    return (jnp.exp(jnp.sum(lq1 * lk1, axis=-1, keepdims=True))
            - jnp.exp(jnp.sum(lq2 * lk2, axis=-1, keepdims=True)) + lam_init)


def _attn_finish(num1, den1, num2, den2, lam, g, lam_init):
    o = num1 / den1 - lam * (num2 / den2)
    ms = jnp.mean(o * o, axis=-1, keepdims=True)
    return o * lax.rsqrt(ms + RMS_EPS) * (g * (1.0 - lam_init))


def _chunk_mask(shape, tq):
    row = lax.broadcasted_iota(jnp.int32, shape, 0)
    col = lax.broadcasted_iota(jnp.int32, shape, 1)
    qc = jnp.where(row >= tq, row - tq, row) // CHUNK_MASK
    return col // CHUNK_MASK <= qc


def _attn_prompt_kernel(q_ref, k_ref, v_ref, lq1, lk1, lq2, lk2, g_ref, o_ref, acc_ref, *, tq, tk, lam_init):
    i = pl.program_id(2)
    qs = _stack_halves(q_ref[...])
    nfull = (i * tq) // tk
    nrest = (i * tq - nfull * tk) // tq

    def tile(start, width, masked):
        rows = pl.ds(pl.multiple_of(start, tq), width)
        p = jnp.exp2(_dg(qs, k_ref[rows, :], NT))
        if masked:
            p = jnp.where(_chunk_mask(p.shape, tq), p, 0.0)
        ve = jnp.concatenate([v_ref[rows, :], jnp.ones((width, HEAD_A), BF16)], axis=1)
        return _dg(p.astype(BF16), ve)

    acc_ref[...] = tile(i * tq, tq, True)

    def rest(j, c):
        acc_ref[...] += tile(nfull * tk + j * tq, tq, False)
        return c

    def full(j, c):
        acc_ref[...] += tile(j * tk, tk, False)
        return c

    lax.fori_loop(0, nrest, rest, 0)
    lax.fori_loop(0, nfull, full, 0)
    acc = acc_ref[...]
    lam = _lambda(lq1[...], lk1[...], lq2[...], lk2[...], lam_init)
    o = _attn_finish(acc[:tq, :HEAD_A], acc[:tq, HEAD_A:], acc[tq:, :HEAD_A], acc[tq:, HEAD_A:],
                     lam, g_ref[...], lam_init)
    o_ref[...] = o.astype(o_ref.dtype)


def _attn_prompt_safe_kernel(q_ref, k_ref, v_ref, lq1, lk1, lq2, lk2, g_ref, o_ref, *, tq, lam_init):
    i = pl.program_id(2)
    qs = _stack_halves(q_ref[...])

    def tile(j, carry, masked):
        m, l, acc = carry
        start = pl.multiple_of(j * tq, tq)
        k = k_ref[pl.ds(start, tq), :]
        v = v_ref[pl.ds(start, tq), :]
        s = _dg(qs, k, NT)
        if masked:
            s = jnp.where(_chunk_mask(s.shape, tq), s, NEG_BIG)
        m_new = jnp.maximum(m, jnp.max(s, axis=-1, keepdims=True))
        p = jnp.exp2(s - m_new)
        alpha = jnp.exp2(m - m_new)
        l = alpha * l + jnp.sum(p, axis=-1, keepdims=True)
        acc = alpha * acc + _dg(p.astype(BF16), v)
        return m_new, l, acc

    init = (jnp.full((2 * tq, 1), NEG_BIG, F32), jnp.zeros((2 * tq, 1), F32),
            jnp.zeros((2 * tq, HEAD_A), F32))
    carry = lax.fori_loop(0, i, lambda j, c: tile(j, c, False), init)
    _, l, acc = tile(i, carry, True)
    lam = _lambda(lq1[...], lk1[...], lq2[...], lk2[...], lam_init)
    o = _attn_finish(acc[:tq], l[:tq], acc[tq:], l[tq:], lam, g_ref[...], lam_init)
    o_ref[...] = o.astype(o_ref.dtype)


def _attn_prompt(q, k, v, lams, g, lam_init, logit_bound, *, tq, tk):
    b, t, width = q.shape
    nh = width // HEAD_A
    small = pl.BlockSpec((1, HALF_A), lambda bi, h, i: (0, 0))

    def call(body, scratch, name, **kw):
        return pl.pallas_call(
            functools.partial(body, tq=tq, lam_init=lam_init, **kw),
            grid=(b, nh, t // tq),
            in_specs=[
                pl.BlockSpec((None, tq, HEAD_A), lambda bi, h, i: (bi, i, h)),
                pl.BlockSpec((None, t, HEAD_A), lambda bi, h, i: (bi, 0, h)),
                pl.BlockSpec((None, t, HEAD_A), lambda bi, h, i: (bi, 0, h)),
                small, small, small, small,
                pl.BlockSpec((1, HEAD_A), lambda bi, h, i: (0, 0)),
            ],
            out_specs=pl.BlockSpec((None, tq, HEAD_A), lambda bi, h, i: (bi, i, h)),
            out_shape=jax.ShapeDtypeStruct((b, t, width), BF16),
            scratch_shapes=scratch,
            compiler_params=_params("parallel", "parallel", "arbitrary"),
            name=name,
        )

    fast = call(_attn_prompt_kernel, [pltpu.VMEM((2 * tq, 2 * HEAD_A), F32)], "attn_prompt", tk=tk)
    safe = call(_attn_prompt_safe_kernel, [], "attn_prompt_safe")
    return lax.cond(logit_bound <= MAX_UNSHIFTED_LOGIT2, fast, safe, q, k, v, *lams, g)


def _attn_sample_kernel(q_ref, kp_ref, vp_ref, kn_ref, vn_ref, lq1, lk1, lq2, lk2, g_ref, o_ref,
                        *, tq, past, nh, lam_init):
    lam = _lambda(lq1[...], lk1[...], lq2[...], lk2[...], lam_init)
    for h in range(nh):
        cols = slice(h * HEAD_A, (h + 1) * HEAD_A)
        head_rows = pl.ds(h, past, stride=nh)
        qs = _stack_halves(q_ref[:, cols])
        s_p = _dg(qs, kp_ref[head_rows, :].astype(BF16), NT)
        s_n = _dg(qs, kn_ref[:, cols], NT)
        row = lax.broadcasted_iota(jnp.int32, s_n.shape, 0)
        col = lax.broadcasted_iota(jnp.int32, s_n.shape, 1)
        qc = (past + jnp.where(row >= tq, row - tq, row)) // CHUNK_MASK
        s_n = jnp.where((past + col) // CHUNK_MASK <= qc, s_n, NEG_BIG)
        m = jnp.maximum(jnp.max(s_p, axis=-1, keepdims=True), jnp.max(s_n, axis=-1, keepdims=True))
        p_p = jnp.exp2(s_p - m)
        p_n = jnp.exp2(s_n - m)
        l = jnp.sum(p_p, axis=-1, keepdims=True) + jnp.sum(p_n, axis=-1, keepdims=True)
        acc = (_dg(p_p.astype(BF16), vp_ref[head_rows, :].astype(BF16))
               + _dg(p_n.astype(BF16), vn_ref[:, cols]))
        o = _attn_finish(acc[:tq], l[:tq], acc[tq:], l[tq:], lam, g_ref[...], lam_init)
        o_ref[:, cols] = o.astype(o_ref.dtype)


def _attn_sample(q, k_past, v_past, layer, k_new, v_new, lams, g, lam_init):
    b, t, width = q.shape
    depth, _, past, nh, _ = k_past.shape
    k_past = k_past.reshape(depth, b, past * nh, HEAD_A)
    v_past = v_past.reshape(depth, b, past * nh, HEAD_A)
    small = pl.BlockSpec((1, HALF_A), lambda bi: (0, 0))
    new = pl.BlockSpec((None, t, width), lambda bi: (bi, 0, 0))
    old = pl.BlockSpec((None, None, past * nh, HEAD_A), lambda bi: (layer, bi, 0, 0))
    return pl.pallas_call(
        functools.partial(_attn_sample_kernel, tq=t, past=past, nh=nh, lam_init=lam_init),
        grid=(b,),
        in_specs=[new, old, old, new, new, small, small, small, small,
                  pl.BlockSpec((1, HEAD_A), lambda bi: (0, 0))],
        out_specs=new,
        out_shape=jax.ShapeDtypeStruct((b, t, width), BF16),
        compiler_params=_params("parallel"),
        name="attn_sample",
    )(q, k_past, v_past, k_new, v_new, *lams, g)


def _stack_heads(x):
    lane = lax.broadcasted_iota(jnp.int32, x.shape, 1)
    return jnp.concatenate([jnp.where(lane < HEAD_B, x, 0.0), jnp.where(lane < HEAD_B, 0.0, x)], axis=0)


def _rwkv_kernel(pr_ref, pk_ref, pv_ref, pl_ref, sr_ref, sk_ref, sv_ref, sl_ref,
                 mr_ref, mk_ref, mv_ref, ml_ref,
                 w0_ref, a0_ref, kk_ref, ka_ref, rk_ref, lg_ref, lb_ref,
                 w2_ref, a2_ref, g2_ref, s0_ref,
                 out_ref, sout_ref,
                 s_sc, cr_sc, ck_sc, cv_sc, cl_sc, *, tb, ng):
    t = pl.program_id(2)
    width = ng * LANES

    @pl.when(t == 0)
    def _():
        s_sc[...] = s0_ref[...]
        cr_sc[...] = sr_ref[...]
        ck_sc[...] = sk_ref[...]
        cv_sc[...] = sv_ref[...]
        cl_sc[...] = sl_ref[...]

    def shifted(p_ref, carry_ref, mu_ref):
        p = p_ref[...]
        prev = pltpu.roll(p, 1, axis=0)
        row = lax.broadcasted_iota(jnp.int32, p.shape, 0)
        prev = jnp.where(row == 0, carry_ref[...], prev)
        carry_ref[...] = p[tb - 1:tb, :]
        return p + (prev - p) * mu_ref[...]

    xr = shifted(pr_ref, cr_sc, mr_ref)
    xk = shifted(pk_ref, ck_sc, mk_ref)
    xv = shifted(pv_ref, cv_sc, mv_ref)
    xl = shifted(pl_ref, cl_sc, ml_ref)

    slab = min(width, 2 * LANES)

    def head_sums(x, value):
        mat = _group_matrix(slab, HEAD_B, value)
        parts = [_mm2(x[:, o:o + slab], mat) for o in range(0, width, slab)]
        return parts[0] if len(parts) == 1 else jnp.concatenate(parts, axis=1)

    z = w0_ref[...] + _dg(jnp.tanh(xl).astype(BF16), w2_ref[...])
    nz = -z
    softplus = jnp.maximum(nz, 0.0) + jnp.log(1.0 + jnp.exp(-jnp.abs(nz)))
    logw = -jnp.exp(-softplus - 0.5)
    ag = _sigmoid(a0_ref[...] + _dg(xl.astype(BF16), a2_ref[...]))
    gate = _dg(_sigmoid(xl).astype(BF16), g2_ref[...])
    kk = xk * kk_ref[...]
    kk = kk / jnp.maximum(jnp.sqrt(head_sums(kk * kk, 1.0)), 1e-12)
    km = xk * (1.0 + (ag - 1.0) * ka_ref[...])
    bonus = head_sums(xr * km * rk_ref[...], 1.0) * xv

    c = CHUNK_B
    c2 = 2 * c
    ri = lax.broadcasted_iota(jnp.int32, (c, c), 0)
    ci = lax.broadcasted_iota(jnp.int32, (c, c), 1)
    tri = jnp.where(ci <= ri, 1.0, 0.0).astype(BF16)
    r2 = lax.broadcasted_iota(jnp.int32, (c2, c2), 0)
    c2i = lax.broadcasted_iota(jnp.int32, (c2, c2), 1)
    rm = r2 % c
    cm = c2i % c
    strict = cm < rm
    incl = cm <= rm
    eye = jnp.where(r2 == c2i, 1.0, 0.0)

    def off_blocks(s):
        rs = rm // s
        return (rs - cm // s) * 2 + rs % 2 == 3

    nchunk = tb // c
    items = [(g, n) for n in range(nchunk) for g in range(ng)]

    pre = {}
    for g, n in items:
        rows = slice(n * c, (n + 1) * c)
        cols = slice(g * LANES, (g + 1) * LANES)
        lw = logw[rows, cols]
        h1 = lw.astype(BF16)
        r1 = lw - h1.astype(F32)
        h2 = r1.astype(BF16)
        h3 = (r1 - h2.astype(F32)).astype(BF16)
        cum = _dg(tri, h1) + (_dg(tri, h2) + _dg(tri, h3))
        w_in = jnp.exp(cum)
        w_ex = jnp.exp(cum - lw)
        w_inv = jnp.exp(-cum)
        kc = kk[rows, cols]
        ar = jnp.concatenate([_stack_heads(-kc * w_ex),
                              _stack_heads(xr[rows, cols] * w_in)], axis=0).astype(BF16)
        bk = jnp.concatenate([_stack_heads(kc * ag[rows, cols] * w_inv),
                              _stack_heads(km[rows, cols] * w_inv)], axis=0).astype(BF16)
        v_s = _stack_heads(xv[rows, cols])
        pre[g, n] = dict(ar=ar, bk=bk, v_h=v_s.astype(BF16), vt_h=v_s.T.astype(BF16), w_last=w_in[c - 1:c, :])

    for key in items:
        d = pre[key]
        gm = _dg(d["ar"], d["bk"], NT)
        d["n_ab"] = jnp.where(strict, gm[:c2, :c2], 0.0).astype(BF16)
        d["n_rb"] = jnp.where(incl, gm[c2:, :c2], 0.0).astype(BF16)
        n_k = jnp.concatenate([jnp.where(strict, gm[:c2, c2:], 0.0),
                               jnp.where(incl, gm[c2:, c2:], 0.0)], axis=0).astype(BF16)
        kv = _dg(n_k, d["v_h"])
        d["akv"] = kv[:c2].astype(BF16)
        d["yv"] = kv[c2:]
        d["vk"] = _dg(d["vt_h"], d["bk"][c2:])

    zero = jnp.zeros((c2, c2), BF16)
    for key in items:
        d = pre[key]
        d["tinv"] = eye + jnp.where(off_blocks(1), d["n_ab"], zero).astype(F32)
    s = 2
    while s < c:
        sel = off_blocks(s)
        for key in items:
            d = pre[key]
            d["tinv_h"] = d["tinv"].astype(BF16)
            d["tmp"] = _dg(jnp.where(sel, d["n_ab"], zero), d["tinv_h"]).astype(BF16)
        for key in items:
            d = pre[key]
            d["tinv"] = d["tinv"] + _dg(d["tinv_h"], d["tmp"])
        s *= 2

    for key in items:
        d = pre[key]
        tx = _dg(d["tinv"].astype(BF16), jnp.concatenate([d["ar"][:c2], d["akv"]], axis=1))
        d["tar"] = jnp.concatenate([tx[:, :c2].astype(BF16), d["ar"][c2:]], axis=0)
        d["u0"] = tx[:, c2:]
        d["u0t"] = tx[:, c2:].T

    state = [s_sc[g] for g in range(ng)]
    ys = {}
    for n in range(nchunk):
        for g in range(ng):
            d = pre[g, n]
            s_h = state[g].astype(BF16)
            ut = _dg(s_h, d["tar"][:c2], NT) + d["u0t"]
            us = _dg(d["tar"], s_h, NT)
            u1 = us[:c2] + d["u0"]
            y_s = us[c2:] + _dg(d["n_rb"], u1.astype(BF16)) + d["yv"]
            ys[g, n] = y_s[:c] + y_s[c:]
            state[g] = (state[g] + _dg(ut.astype(BF16), d["bk"][:c2]) + d["vk"]) * d["w_last"]
    for g in range(ng):
        s_sc[g] = state[g]

    @pl.when(t == pl.num_programs(2) - 1)
    def _():
        for g in range(ng):
            sout_ref[g] = state[g]

    y = jnp.concatenate(
        [jnp.concatenate([ys[g, n] for n in range(nchunk)], axis=0) for g in range(ng)], axis=1)
    mean = head_sums(y, 1.0 / HEAD_B)
    dlt = y - mean
    var = head_sums(dlt * dlt, 1.0 / HEAD_B)
    yn = dlt * lax.rsqrt(var + GN_EPS) * lg_ref[...] + lb_ref[...]
    out_ref[...] = ((yn + bonus) * gate).astype(out_ref.dtype)


def _rwkv(p, shift0, s0_bd, mu, vecs, loras, *, tb, ng):
    b, t, _ = p.shape
    width = vecs[0].shape[-1]
    wblk = ng * LANES
    nblk = width // wblk
    lcol = 3 * width // (2 * LANES)

    def col(off):
        return lambda bi, g, ti: (bi, ti, off + g)

    def row(off):
        return lambda bi, g, ti: (bi, 0, off + g)

    def par(off):
        return lambda bi, g, ti: (0, off + g)

    in_specs = (
        [pl.BlockSpec((None, tb, wblk), col(k * nblk)) for k in range(3)]
        + [pl.BlockSpec((None, tb, 2 * LANES), lambda bi, g, ti: (bi, ti, lcol))]
        + [pl.BlockSpec((None, 1, wblk), row(k * nblk)) for k in range(3)]
        + [pl.BlockSpec((None, 1, 2 * LANES), lambda bi, g, ti: (bi, 0, lcol))]
        + [pl.BlockSpec((1, wblk), par(k * nblk)) for k in range(3)]
        + [pl.BlockSpec((1, 2 * LANES), lambda bi, g, ti: (0, lcol))]
        + [pl.BlockSpec((1, wblk), par(0)) for _ in vecs]
        + [pl.BlockSpec((2 * LANES, wblk), par(0)) for _ in loras]
        + [pl.BlockSpec((None, ng, LANES, LANES), lambda bi, g, ti: (bi, g, 0, 0))]
    )
    return pl.pallas_call(
        functools.partial(_rwkv_kernel, tb=tb, ng=ng),
        grid=(b, nblk, t // tb),
        in_specs=in_specs,
        out_specs=[pl.BlockSpec((None, tb, wblk), col(0)),
                   pl.BlockSpec((None, ng, LANES, LANES), lambda bi, g, ti: (bi, g, 0, 0))],
        out_shape=[jax.ShapeDtypeStruct((b, t, width), BF16),
                   jax.ShapeDtypeStruct((b, width // LANES, LANES, LANES), F32)],
        scratch_shapes=[pltpu.VMEM((ng, LANES, LANES), F32), pltpu.VMEM((1, wblk), F32),
                        pltpu.VMEM((1, wblk), F32), pltpu.VMEM((1, wblk), F32),
                        pltpu.VMEM((1, 2 * LANES), F32)],
        compiler_params=_params("parallel", "parallel", "arbitrary"),
        name="rwkv",
    )(p, p, p, p, shift0, shift0, shift0, shift0, mu, mu, mu, mu, *vecs, *loras, s0_bd)


def _merge_ffn_kernel(x_ref, a_ref, b_ref, wo_ref, g_ref, wg_ref, wu_ref, wd_ref, o_ref, hn_ref):
    @pl.when(pl.program_id(1) == 0)
    def _():
        half = a_ref.shape[1]
        h = x_ref[...] + (_dg(a_ref[...], wo_ref[:half, :]) + _dg(b_ref[...], wo_ref[half:, :]))
        ms = jnp.mean(h * h, axis=-1, keepdims=True)
        hn_ref[...] = (h * lax.rsqrt(ms + RMS_EPS) * g_ref[...]).astype(BF16)
        o_ref[...] = h

    hn = hn_ref[...]
    gt = _dg(hn, wg_ref[...])
    up = _dg(hn, wu_ref[...])
    act = gt * _sigmoid(gt) * up
    o_ref[...] += _dg(act.astype(BF16), wd_ref[...])


def _merge_ffn(x2d, a, b, wo, g, wg, wu, wd, *, tm, tf):
    n, d = x2d.shape
    half = a.shape[1]
    dff = wg.shape[1]
    return pl.pallas_call(
        _merge_ffn_kernel,
        grid=(n // tm, dff // tf),
        in_specs=[pl.BlockSpec((tm, d), lambda i, j: (i, 0)),
                  pl.BlockSpec((tm, half), lambda i, j: (i, 0)),
                  pl.BlockSpec((tm, half), lambda i, j: (i, 0)),
                  pl.BlockSpec((d, d), lambda i, j: (0, 0), pipeline_mode=pl.Buffered(1)),
                  pl.BlockSpec((1, d), lambda i, j: (0, 0)),
                  pl.BlockSpec((d, tf), lambda i, j: (0, j)),
                  pl.BlockSpec((d, tf), lambda i, j: (0, j)),
                  pl.BlockSpec((tf, d), lambda i, j: (j, 0))],
        out_specs=pl.BlockSpec((tm, d), lambda i, j: (i, 0)),
        out_shape=jax.ShapeDtypeStruct((n, d), F32),
        scratch_shapes=[pltpu.VMEM((tm, d), BF16)],
        compiler_params=_params("parallel", "arbitrary"),
        name="merge_ffn",
    )(x2d, a, b, wo, g.reshape(1, d), wg, wu, wd)


def _pick(n, pref):
    return pref if n % pref == 0 else n


def _block_diag_state(s):
    b, h = s.shape[:2]
    s = s.reshape(b, h // 2, 2, HEAD_B, HEAD_B)
    z = jnp.zeros_like(s[:, :, 0])
    top = jnp.concatenate([s[:, :, 0], z], axis=-1)
    bot = jnp.concatenate([z, s[:, :, 1]], axis=-1)
    return jnp.concatenate([top, bot], axis=-2)


def _unblock_state(s_bd):
    b, g = s_bd.shape[:2]
    s = jnp.stack([s_bd[:, :, :HEAD_B, :HEAD_B], s_bd[:, :, HEAD_B:, HEAD_B:]], axis=2)
    return s.reshape(b, 2 * g, HEAD_B, HEAD_B)


def _layer(x, pos, past, s0, shift0, lam_init, wts):
    b, t, d = x.shape
    n = b * t
    wa = wts["width_a"]
    wb = wts["width_b"]
    x2d = x.reshape(n, d)
    tm = _pick(n, 512)
    tm_in = _pick(n, 1024)
    rope = _rope_tables(pos, max(t, tm_in))
    lams = wts["lams"]

    q, k32, k16, v32, v16, p_rwkv = _inproj(
        x2d, wts["norm1_g"], wts["w_in"], wts["q_norm_g"], wts["k_norm_g"], rope,
        width_a=wa, q_scale=HALF_A ** -0.5 * LOG2E, tm=tm_in, tn=INPROJ_TN)

    q = q.reshape(b, t, wa)
    k16 = k16.reshape(b, t, wa)
    v16 = v16.reshape(b, t, wa)
    if past is None:
        bound = (8.0 * 1.01 * LOG2E) * jnp.max(jnp.abs(wts["q_norm_g"])) * jnp.max(jnp.abs(wts["k_norm_g"]))
        attn = _attn_prompt(q, k16, v16, lams, wts["subln_g"], lam_init, bound,
                            tq=_pick(t, 512), tk=_pick(t, 2048))
        tb, ng = _pick(t, 256), 4
    else:
        attn = _attn_sample(q, *past, k16, v16, lams, wts["subln_g"], lam_init)
        tb, ng = _pick(t, CHUNK_B), wb // LANES

    p_rwkv = p_rwkv.reshape(b, t, -1)
    rw, s_bd = _rwkv(p_rwkv, shift0, s0, wts["mu"], wts["vecs"], wts["loras"], tb=tb, ng=ng)

    y = _merge_ffn(x2d, attn.reshape(n, wa), rw.reshape(n, wb), wts["w_out"], wts["norm2_g"],
                   wts["w_gate"], wts["w_up"], wts["w_down"], tm=tm, tf=512)

    nh_a = wa // HEAD_A
    rcols = wts["rwkv_cols"]
    return (y.reshape(b, t, d), k32.reshape(b, t, nh_a, HEAD_A), v32.reshape(b, t, nh_a, HEAD_A),
            _unblock_state(s_bd), p_rwkv[:, t - 1:, :rcols])


def kernel(x_prompt, x_sample, cache_attn_k, cache_attn_v, state_rwkv, state_rwkv_shift, norm1_g, w_in, q_norm_g, k_norm_g, lambda_q1, lambda_k1, lambda_q2, lambda_k2, subln_g, mu_rwkv, w0, w2, a0, a2, g2, k_k, k_a, r_k, lnx_g, lnx_b, w_out, norm2_g, w_gate, w_up, w_down):
    depth = w_in.shape[0]
    bp, tp, d = x_prompt.shape
    bs, ts, _ = x_sample.shape
    past_len = cache_attn_k.shape[2]
    nh_a = cache_attn_k.shape[3]
    wa = nh_a * HEAD_A
    wb = d - wa
    rcols = 3 * wb + 3 * LORA
    pad = -rcols % INPROJ_TN
    pos_p = jnp.arange(tp, dtype=jnp.int32)
    pos_s = past_len + jnp.arange(ts, dtype=jnp.int32)

    xp, xs = x_prompt, x_sample
    outs = [[] for _ in range(8)]
    for l in range(depth):
        lam_init = 0.8 - 0.6 * math.exp(-0.3 * l)
        wl = w_in[l].astype(BF16)

        def lora_pad(w, slot):
            return jnp.pad(w, ((slot * LORA, 2 * LANES - (slot + 1) * LORA), (0, 0)))

        row = lambda v: v.reshape(1, -1)
        wts = dict(
            width_a=wa, width_b=wb, rwkv_cols=rcols,
            norm1_g=norm1_g[l], norm2_g=norm2_g[l],
            w_in=jnp.pad(wl, ((0, 0), (0, pad))),
            q_norm_g=row(q_norm_g[l]), k_norm_g=row(k_norm_g[l]),
            lams=[row(lambda_q1[l]), row(lambda_k1[l]), row(lambda_q2[l]), row(lambda_k2[l])],
            subln_g=row(subln_g[l]),
            mu=jnp.pad(row(mu_rwkv[l]), ((0, 0), (0, pad))),
            vecs=[row(w0[l]), row(a0[l]), row(k_k[l]), row(k_a[l]), row(r_k[l]), row(lnx_g[l]), row(lnx_b[l])],
            loras=[lora_pad(w2[l], 0).astype(BF16), lora_pad(a2[l], 1).astype(BF16),
                   lora_pad(g2[l], 2).astype(BF16)],
            w_out=w_out[l].astype(BF16), w_gate=w_gate[l].astype(BF16),
            w_up=w_up[l].astype(BF16), w_down=w_down[l].astype(BF16),
        )
        s0_p = jnp.zeros((bp, wb // LANES, LANES, LANES), F32)
        shift0_p = jnp.zeros((bp, 1, rcols + pad), F32)
        xp, kp, vp, sp, shp = _layer(xp, pos_p, None, s0_p, shift0_p, lam_init, wts)
        past = (cache_attn_k, cache_attn_v, l)
        shift0_s = jnp.pad(state_rwkv_shift[l], ((0, 0), (0, 0), (0, pad)))
        xs, ks, vs, ss, shs = _layer(xs, pos_s, past, _block_diag_state(state_rwkv[l]), shift0_s,
                                     lam_init, wts)
        for lst, val in zip(outs, (kp, vp, sp, shp, ks, vs, ss, shs)):
            lst.append(val)
    return (xp, xs) + tuple(jnp.stack(o) for o in outs)
```

```python
import functools
import math

import jax
import jax.numpy as jnp
from jax import lax
from jax.experimental import pallas as pl
from jax.experimental.pallas import tpu as pltpu

F32 = jnp.float32
BF16 = jnp.bfloat16

LANES = 128
HEAD_A = 128
HALF_A = 64
ROT_HALF = 8
ROPE_THETA = 500000.0
HEAD_B = 64
CHUNK_MASK = 64
CHUNK_B = 64
LORA = 64
RMS_EPS = 1e-6
GN_EPS = 64e-5
NEG_BIG = -1e30
LOG2E = 1.4426950408889634
MAX_UNSHIFTED_LOGIT2 = 64.0
VMEM_LIMIT = 56 * 1024 * 1024
INPROJ_TN = 512

NN = ((1,), (0,))
NT = ((1,), (1,))


def _dg(a, b, dims=NN):
    return lax.dot_general(a, b, (dims, ((), ())), preferred_element_type=F32)


def _split(a):
    hi = a.astype(BF16)
    lo = (a - hi.astype(F32)).astype(BF16)
    return hi, lo


def _mm2(a, b_exact):
    ah, al = _split(a)
    return _dg(ah, b_exact) + _dg(al, b_exact)


def _group_matrix(n, group, value):
    r = lax.broadcasted_iota(jnp.int32, (n, n), 0) // group
    c = lax.broadcasted_iota(jnp.int32, (n, n), 1) // group
    return jnp.where(r == c, value, 0.0).astype(BF16)


def _sigmoid(x):
    return 1.0 / (1.0 + jnp.exp(-x))


def _params(*sem):
    return pltpu.CompilerParams(dimension_semantics=sem, vmem_limit_bytes=VMEM_LIMIT)


def _inproj_kernel(x_ref, g_ref, w_ref, qg_ref, kg_ref, cos_ref, sin_ref,
                   q_ref, k32_ref, k16_ref, v32_ref, v16_ref, p_ref, xn_ref, *, nsec, q_scale):
    j = pl.program_id(1)

    @pl.when(j == 0)
    def _():
        x = x_ref[...]
        ms = jnp.mean(x * x, axis=-1, keepdims=True)
        xn_ref[...] = (x * lax.rsqrt(ms + RMS_EPS) * g_ref[...]).astype(BF16)

    acc = _dg(xn_ref[...], w_ref[...])

    @pl.when(j < 2 * nsec)
    def _():
        is_q = j < nsec
        seg = _group_matrix(LANES, HALF_A, 1.0 / HALF_A)
        hg = jnp.where(is_q, qg_ref[...] * q_scale, kg_ref[...])
        cos, sin = cos_ref[...], sin_ref[...]
        lane = lax.broadcasted_iota(jnp.int32, sin.shape, 1) % HALF_A
        sa = jnp.where(lane < ROT_HALF, sin, 0.0)
        sb = jnp.where(lane < ROT_HALF, 0.0, sin)
        groups = []
        for g in range(acc.shape[1] // LANES):
            xg = acc[:, g * LANES:(g + 1) * LANES]
            ms = _mm2(xg * xg, seg)
            xg = xg * lax.rsqrt(ms + RMS_EPS) * hg
            up = pltpu.roll(xg, LANES - ROT_HALF, axis=1)
            dn = pltpu.roll(xg, ROT_HALF, axis=1)
            groups.append(xg * cos + up * sa + dn * sb)
        rot = jnp.concatenate(groups, axis=1)

        @pl.when(is_q)
        def _():
            q_ref[...] = rot.astype(BF16)

        @pl.when(jnp.logical_not(is_q))
        def _():
            k32_ref[...] = rot
            k16_ref[...] = rot.astype(BF16)

    @pl.when(jnp.logical_and(j >= 2 * nsec, j < 3 * nsec))
    def _():
        v32_ref[...] = acc
        v16_ref[...] = acc.astype(BF16)

    @pl.when(j >= 3 * nsec)
    def _():
        p_ref[...] = acc


def _inproj(x2d, g, w, qg, kg, rope, *, width_a, q_scale, tm, tn):
    n, d = x2d.shape
    ncol = w.shape[1]
    nsec = width_a // tn
    nblk = rope[0].shape[0] // tm
    sec = lambda k: (lambda i, j: (i, jnp.clip(j - k * nsec, 0, nsec - 1)))
    vec = pl.BlockSpec((1, LANES), lambda i, j: (0, 0))
    tab = pl.BlockSpec((tm, LANES), lambda i, j: (i % nblk, 0))
    blk = lambda imap: pl.BlockSpec((tm, tn), imap)
    return pl.pallas_call(
        functools.partial(_inproj_kernel, nsec=nsec, q_scale=q_scale),
        grid=(n // tm, ncol // tn),
        in_specs=[pl.BlockSpec((tm, d), lambda i, j: (i, 0)),
                  pl.BlockSpec((1, d), lambda i, j: (0, 0)),
                  pl.BlockSpec((d, tn), lambda i, j: (0, j)),
                  vec, vec, tab, tab],
        out_specs=[blk(sec(0)), blk(sec(1)), blk(sec(1)), blk(sec(2)), blk(sec(2)),
                   blk(lambda i, j: (i, jnp.maximum(j - 3 * nsec, 0)))],
        out_shape=[jax.ShapeDtypeStruct((n, width_a), BF16),
                   jax.ShapeDtypeStruct((n, width_a), F32), jax.ShapeDtypeStruct((n, width_a), BF16),
                   jax.ShapeDtypeStruct((n, width_a), F32), jax.ShapeDtypeStruct((n, width_a), BF16),
                   jax.ShapeDtypeStruct((n, ncol - 3 * width_a), F32)],
        scratch_shapes=[pltpu.VMEM((tm, d), BF16)],
        compiler_params=_params("parallel", "arbitrary"),
        name="inproj",
    )(x2d, g.reshape(1, d), w, qg, kg, *rope)


def _rope_tables(pos, rows):
    inv = jnp.float32(ROPE_THETA) ** (-jnp.arange(0, 2 * ROT_HALF, 2, dtype=F32) / (2 * ROT_HALF))
    lane = jnp.arange(HEAD_A) % HALF_A
    inv_lane = jnp.where(lane < 2 * ROT_HALF, inv[lane % ROT_HALF], 0.0)
    sign = jnp.where(lane < ROT_HALF, -1.0, 1.0)
    ang = pos.astype(F32)[:, None] * inv_lane[None, :]
    tabs = [jnp.cos(ang), jnp.sin(ang) * sign]
    t = pos.shape[0]
    if rows > t:
        tabs = [jnp.tile(h, (rows // t, 1)) for h in tabs]
    return tabs


def _stack_halves(q):
    lane = lax.broadcasted_iota(jnp.int32, q.shape, 1)
    zero = jnp.zeros_like(q)
    return jnp.concatenate([jnp.where(lane < HALF_A, q, zero), jnp.where(lane < HALF_A, zero, q)], axis=0)


def _lambda(lq1, lk1, lq2, lk2, lam_init):
    return (jnp.exp(jnp.sum(lq1 * lk1, axis=-1, keepdims=True))
            - jnp.exp(jnp.sum(lq2 * lk2, axis=-1, keepdims=True)) + lam_init)


def _attn_finish(num1, den1, num2, den2, lam, g, lam_init):
    o = num1 / den1 - lam * (num2 / den2)
    ms = jnp.mean(o * o, axis=-1, keepdims=True)
    return o * lax.rsqrt(ms + RMS_EPS) * (g * (1.0 - lam_init))


def _chunk_mask(shape, tq):
    row = lax.broadcasted_iota(jnp.int32, shape, 0)
    col = lax.broadcasted_iota(jnp.int32, shape, 1)
    qc = jnp.where(row >= tq, row - tq, row) // CHUNK_MASK
    return col // CHUNK_MASK <= qc


def _attn_prompt_kernel(q_ref, k_ref, v_ref, lq1, lk1, lq2, lk2, g_ref, o_ref, acc_ref, *, tq, tk, lam_init):
    i = pl.program_id(2)
    qs = _stack_halves(q_ref[...])
    nfull = (i * tq) // tk
    nrest = (i * tq - nfull * tk) // tq

    def tile(start, width, masked):
        rows = pl.ds(pl.multiple_of(start, tq), width)
        p = jnp.exp2(_dg(qs, k_ref[rows, :], NT))
        if masked:
            p = jnp.where(_chunk_mask(p.shape, tq), p, 0.0)
        ve = jnp.concatenate([v_ref[rows, :], jnp.ones((width, HEAD_A), BF16)], axis=1)
        return _dg(p.astype(BF16), ve)

    acc_ref[...] = tile(i * tq, tq, True)

    def rest(j, c):
        acc_ref[...] += tile(nfull * tk + j * tq, tq, False)
        return c

    def full(j, c):
        acc_ref[...] += tile(j * tk, tk, False)
        return c

    lax.fori_loop(0, nrest, rest, 0)
    lax.fori_loop(0, nfull, full, 0)
    acc = acc_ref[...]
    lam = _lambda(lq1[...], lk1[...], lq2[...], lk2[...], lam_init)
    o = _attn_finish(acc[:tq, :HEAD_A], acc[:tq, HEAD_A:], acc[tq:, :HEAD_A], acc[tq:, HEAD_A:],
                     lam, g_ref[...], lam_init)
    o_ref[...] = o.astype(o_ref.dtype)


def _attn_prompt_safe_kernel(q_ref, k_ref, v_ref, lq1, lk1, lq2, lk2, g_ref, o_ref, *, tq, lam_init):
    i = pl.program_id(2)
    qs = _stack_halves(q_ref[...])

    def tile(j, carry, masked):
        m, l, acc = carry
        start = pl.multiple_of(j * tq, tq)
        k = k_ref[pl.ds(start, tq), :]
        v = v_ref[pl.ds(start, tq), :]
        s = _dg(qs, k, NT)
        if masked:
            s = jnp.where(_chunk_mask(s.shape, tq), s, NEG_BIG)
        m_new = jnp.maximum(m, jnp.max(s, axis=-1, keepdims=True))
        p = jnp.exp2(s - m_new)
        alpha = jnp.exp2(m - m_new)
        l = alpha * l + jnp.sum(p, axis=-1, keepdims=True)
        acc = alpha * acc + _dg(p.astype(BF16), v)
        return m_new, l, acc

    init = (jnp.full((2 * tq, 1), NEG_BIG, F32), jnp.zeros((2 * tq, 1), F32),
            jnp.zeros((2 * tq, HEAD_A), F32))
    carry = lax.fori_loop(0, i, lambda j, c: tile(j, c, False), init)
    _, l, acc = tile(i, carry, True)
    lam = _lambda(lq1[...], lk1[...], lq2[...], lk2[...], lam_init)
    o = _attn_finish(acc[:tq], l[:tq], acc[tq:], l[tq:], lam, g_ref[...], lam_init)
    o_ref[...] = o.astype(o_ref.dtype)


def _attn_prompt(q, k, v, lams, g, lam_init, logit_bound, *, tq, tk):
    b, t, width = q.shape
    nh = width // HEAD_A
    small = pl.BlockSpec((1, HALF_A), lambda bi, h, i: (0, 0))

    def call(body, scratch, name, **kw):
        return pl.pallas_call(
            functools.partial(body, tq=tq, lam_init=lam_init, **kw),
            grid=(b, nh, t // tq),
            in_specs=[
                pl.BlockSpec((None, tq, HEAD_A), lambda bi, h, i: (bi, i, h)),
                pl.BlockSpec((None, t, HEAD_A), lambda bi, h, i: (bi, 0, h)),
                pl.BlockSpec((None, t, HEAD_A), lambda bi, h, i: (bi, 0, h)),
                small, small, small, small,
                pl.BlockSpec((1, HEAD_A), lambda bi, h, i: (0, 0)),
            ],
            out_specs=pl.BlockSpec((None, tq, HEAD_A), lambda bi, h, i: (bi, i, h)),
            out_shape=jax.ShapeDtypeStruct((b, t, width), BF16),
            scratch_shapes=scratch,
            compiler_params=_params("parallel", "parallel", "arbitrary"),
            name=name,
        )

    fast = call(_attn_prompt_kernel, [pltpu.VMEM((2 * tq, 2 * HEAD_A), F32)], "attn_prompt", tk=tk)
    safe = call(_attn_prompt_safe_kernel, [], "attn_prompt_safe")
    return lax.cond(logit_bound <= MAX_UNSHIFTED_LOGIT2, fast, safe, q, k, v, *lams, g)


def _attn_sample_kernel(q_ref, kp_ref, vp_ref, kn_ref, vn_ref, lq1, lk1, lq2, lk2, g_ref, o_ref,
                        *, tq, past, nh, lam_init):
    lam = _lambda(lq1[...], lk1[...], lq2[...], lk2[...], lam_init)
    for h in range(nh):
        cols = slice(h * HEAD_A, (h + 1) * HEAD_A)
        head_rows = pl.ds(h, past, stride=nh)
        qs = _stack_halves(q_ref[:, cols])
        s_p = _dg(qs, kp_ref[head_rows, :].astype(BF16), NT)
        s_n = _dg(qs, kn_ref[:, cols], NT)
        row = lax.broadcasted_iota(jnp.int32, s_n.shape, 0)
        col = lax.broadcasted_iota(jnp.int32, s_n.shape, 1)
        qc = (past + jnp.where(row >= tq, row - tq, row)) // CHUNK_MASK
        s_n = jnp.where((past + col) // CHUNK_MASK <= qc, s_n, NEG_BIG)
        m = jnp.maximum(jnp.max(s_p, axis=-1, keepdims=True), jnp.max(s_n, axis=-1, keepdims=True))
        p_p = jnp.exp2(s_p - m)
        p_n = jnp.exp2(s_n - m)
        l = jnp.sum(p_p, axis=-1, keepdims=True) + jnp.sum(p_n, axis=-1, keepdims=True)
        acc = (_dg(p_p.astype(BF16), vp_ref[head_rows, :].astype(BF16))
               + _dg(p_n.astype(BF16), vn_ref[:, cols]))
        o = _attn_finish(acc[:tq], l[:tq], acc[tq:], l[tq:], lam, g_ref[...], lam_init)
        o_ref[:, cols] = o.astype(o_ref.dtype)


def _attn_sample(q, k_past, v_past, layer, k_new, v_new, lams, g, lam_init):
    b, t, width = q.shape
    depth, _, past, nh, _ = k_past.shape
    k_past = k_past.reshape(depth, b, past * nh, HEAD_A)
    v_past = v_past.reshape(depth, b, past * nh, HEAD_A)
    small = pl.BlockSpec((1, HALF_A), lambda bi: (0, 0))
    new = pl.BlockSpec((None, t, width), lambda bi: (bi, 0, 0))
    old = pl.BlockSpec((None, None, past * nh, HEAD_A), lambda bi: (layer, bi, 0, 0))
    return pl.pallas_call(
        functools.partial(_attn_sample_kernel, tq=t, past=past, nh=nh, lam_init=lam_init),
        grid=(b,),
        in_specs=[new, old, old, new, new, small, small, small, small,
                  pl.BlockSpec((1, HEAD_A), lambda bi: (0, 0))],
        out_specs=new,
        out_shape=jax.ShapeDtypeStruct((b, t, width), BF16),
        compiler_params=_params("parallel"),
        name="attn_sample",
    )(q, k_past, v_past, k_new, v_new, *lams, g)


def _stack_heads(x):
    lane = lax.broadcasted_iota(jnp.int32, x.shape, 1)
    return jnp.concatenate([jnp.where(lane < HEAD_B, x, 0.0), jnp.where(lane < HEAD_B, 0.0, x)], axis=0)


def _rwkv_kernel(pr_ref, pk_ref, pv_ref, pl_ref, sr_ref, sk_ref, sv_ref, sl_ref,
                 mr_ref, mk_ref, mv_ref, ml_ref,
                 w0_ref, a0_ref, kk_ref, ka_ref, rk_ref, lg_ref, lb_ref,
                 w2_ref, a2_ref, g2_ref, s0_ref,
                 out_ref, sout_ref,
                 s_sc, cr_sc, ck_sc, cv_sc, cl_sc, *, tb, ng):
    t = pl.program_id(2)
    width = ng * LANES

    @pl.when(t == 0)
    def _():
        s_sc[...] = s0_ref[...]
        cr_sc[...] = sr_ref[...]
        ck_sc[...] = sk_ref[...]
        cv_sc[...] = sv_ref[...]
        cl_sc[...] = sl_ref[...]

    def shifted(p_ref, carry_ref, mu_ref):
        p = p_ref[...]
        prev = pltpu.roll(p, 1, axis=0)
        row = lax.broadcasted_iota(jnp.int32, p.shape, 0)
        prev = jnp.where(row == 0, carry_ref[...], prev)
        carry_ref[...] = p[tb - 1:tb, :]
        return p + (prev - p) * mu_ref[...]

    xr = shifted(pr_ref, cr_sc, mr_ref)
    xk = shifted(pk_ref, ck_sc, mk_ref)
    xv = shifted(pv_ref, cv_sc, mv_ref)
    xl = shifted(pl_ref, cl_sc, ml_ref)

    slab = min(width, 2 * LANES)

    def head_sums(x, value):
        mat = _group_matrix(slab, HEAD_B, value)
        parts = [_mm2(x[:, o:o + slab], mat) for o in range(0, width, slab)]
        return parts[0] if len(parts) == 1 else jnp.concatenate(parts, axis=1)

    z = w0_ref[...] + _dg(jnp.tanh(xl).astype(BF16), w2_ref[...])
    nz = -z
    softplus = jnp.maximum(nz, 0.0) + jnp.log(1.0 + jnp.exp(-jnp.abs(nz)))
    logw = -jnp.exp(-softplus - 0.5)
    ag = _sigmoid(a0_ref[...] + _dg(xl.astype(BF16), a2_ref[...]))
    gate = _dg(_sigmoid(xl).astype(BF16), g2_ref[...])
    kk = xk * kk_ref[...]
    kk = kk / jnp.maximum(jnp.sqrt(head_sums(kk * kk, 1.0)), 1e-12)
    km = xk * (1.0 + (ag - 1.0) * ka_ref[...])
    bonus = head_sums(xr * km * rk_ref[...], 1.0) * xv

    c = CHUNK_B
    c2 = 2 * c
    ri = lax.broadcasted_iota(jnp.int32, (c, c), 0)
    ci = lax.broadcasted_iota(jnp.int32, (c, c), 1)
    tri = jnp.where(ci <= ri, 1.0, 0.0).astype(BF16)
    r2 = lax.broadcasted_iota(jnp.int32, (c2, c2), 0)
    c2i = lax.broadcasted_iota(jnp.int32, (c2, c2), 1)
    rm = r2 % c
    cm = c2i % c
    strict = cm < rm
    incl = cm <= rm
    eye = jnp.where(r2 == c2i, 1.0, 0.0)

    def off_blocks(s):
        rs = rm // s
        return (rs - cm // s) * 2 + rs % 2 == 3

    nchunk = tb // c
    zero = jnp.zeros((c2, c2), BF16)
    pre = {}

    def prepare(keys):
        for g, n in keys:
            rows = slice(n * c, (n + 1) * c)
            cols = slice(g * LANES, (g + 1) * LANES)
            lw = logw[rows, cols]
            h1 = lw.astype(BF16)
            r1 = lw - h1.astype(F32)
            h2 = r1.astype(BF16)
            h3 = (r1 - h2.astype(F32)).astype(BF16)
            cum = _dg(tri, h1) + (_dg(tri, h2) + _dg(tri, h3))
            w_in = jnp.exp(cum)
            w_ex = jnp.exp(cum - lw)
            w_inv = jnp.exp(-cum)
            kc = kk[rows, cols]
            ar = jnp.concatenate([_stack_heads(-kc * w_ex),
                                  _stack_heads(xr[rows, cols] * w_in)], axis=0).astype(BF16)
            bk = jnp.concatenate([_stack_heads(kc * ag[rows, cols] * w_inv),
                                  _stack_heads(km[rows, cols] * w_inv)], axis=0).astype(BF16)
            v_s = _stack_heads(xv[rows, cols])
            pre[g, n] = dict(ar=ar, bk=bk, v_h=v_s.astype(BF16), vt_h=v_s.T.astype(BF16),
                             w_last=w_in[c - 1:c, :])

        for key in keys:
            d = pre[key]
            gm = _dg(d["ar"], d["bk"], NT)
            d["n_ab"] = jnp.where(strict, gm[:c2, :c2], 0.0).astype(BF16)
            d["n_rb"] = jnp.where(incl, gm[c2:, :c2], 0.0).astype(BF16)
            n_k = jnp.concatenate([jnp.where(strict, gm[:c2, c2:], 0.0),
                                   jnp.where(incl, gm[c2:, c2:], 0.0)], axis=0).astype(BF16)
            kv = _dg(n_k, d["v_h"])
            d["akv"] = kv[:c2].astype(BF16)
            d["yv"] = kv[c2:]
            d["vk"] = _dg(d["vt_h"], d["bk"][c2:])

        for key in keys:
            d = pre[key]
            d["tinv"] = eye + jnp.where(off_blocks(1), d["n_ab"], zero).astype(F32)
        s = 2
        while s < c:
            sel = off_blocks(s)
            for key in keys:
                d = pre[key]
                d["tinv_h"] = d["tinv"].astype(BF16)
                d["tmp"] = _dg(jnp.where(sel, d["n_ab"], zero), d["tinv_h"]).astype(BF16)
            for key in keys:
                d = pre[key]
                d["tinv"] = d["tinv"] + _dg(d["tinv_h"], d["tmp"])
            s *= 2

        for key in keys:
            d = pre[key]
            tx = _dg(d["tinv"].astype(BF16), jnp.concatenate([d["ar"][:c2], d["akv"]], axis=1))
            d["tar"] = jnp.concatenate([tx[:, :c2].astype(BF16), d["ar"][c2:]], axis=0)
            d["u0"] = tx[:, c2:]
            d["u0t"] = tx[:, c2:].T

    state = [s_sc[g] for g in range(ng)]
    ys = {}

    def advance(n):
        for g in range(ng):
            d = pre[g, n]
            s_h = state[g].astype(BF16)
            d["ut"] = _dg(s_h, d["tar"][:c2], NT) + d["u0t"]
            d["us"] = _dg(d["tar"], s_h, NT)
        for g in range(ng):
            d = pre[g, n]
            u1 = d["us"][:c2] + d["u0"]
            y_s = d["us"][c2:] + _dg(d["n_rb"], u1.astype(BF16)) + d["yv"]
            ys[g, n] = y_s[:c] + y_s[c:]
            state[g] = (state[g] + _dg(d["ut"].astype(BF16), d["bk"][:c2]) + d["vk"]) * d["w_last"]

    prepare([(g, n) for n in range(nchunk) for g in range(ng)])
    for n in range(nchunk):
        advance(n)
    for g in range(ng):
        s_sc[g] = state[g]

    @pl.when(t == pl.num_programs(2) - 1)
    def _():
        for g in range(ng):
            sout_ref[g] = state[g]

    y = jnp.concatenate(
        [jnp.concatenate([ys[g, n] for n in range(nchunk)], axis=0) for g in range(ng)], axis=1)
    mean = head_sums(y, 1.0 / HEAD_B)
    dlt = y - mean
    var = head_sums(dlt * dlt, 1.0 / HEAD_B)
    yn = dlt * lax.rsqrt(var + GN_EPS) * lg_ref[...] + lb_ref[...]
    out_ref[...] = ((yn + bonus) * gate).astype(out_ref.dtype)


def _rwkv(p, shift0, s0_bd, mu, vecs, loras, *, tb, ng):
    b, t, _ = p.shape
    width = vecs[0].shape[-1]
    wblk = ng * LANES
    nblk = width // wblk
    lcol = 3 * width // (2 * LANES)

    def col(off):
        return lambda bi, g, ti: (bi, ti, off + g)

    def row(off):
        return lambda bi, g, ti: (bi, 0, off + g)

    def par(off):
        return lambda bi, g, ti: (0, off + g)

    in_specs = (
        [pl.BlockSpec((None, tb, wblk), col(k * nblk)) for k in range(3)]
        + [pl.BlockSpec((None, tb, 2 * LANES), lambda bi, g, ti: (bi, ti, lcol))]
        + [pl.BlockSpec((None, 1, wblk), row(k * nblk)) for k in range(3)]
        + [pl.BlockSpec((None, 1, 2 * LANES), lambda bi, g, ti: (bi, 0, lcol))]
        + [pl.BlockSpec((1, wblk), par(k * nblk)) for k in range(3)]
        + [pl.BlockSpec((1, 2 * LANES), lambda bi, g, ti: (0, lcol))]
        + [pl.BlockSpec((1, wblk), par(0)) for _ in vecs]
        + [pl.BlockSpec((2 * LANES, wblk), par(0)) for _ in loras]
        + [pl.BlockSpec((None, ng, LANES, LANES), lambda bi, g, ti: (bi, g, 0, 0))]
    )
    return pl.pallas_call(
        functools.partial(_rwkv_kernel, tb=tb, ng=ng),
        grid=(b, nblk, t // tb),
        in_specs=in_specs,
        out_specs=[pl.BlockSpec((None, tb, wblk), col(0)),
                   pl.BlockSpec((None, ng, LANES, LANES), lambda bi, g, ti: (bi, g, 0, 0))],
        out_shape=[jax.ShapeDtypeStruct((b, t, width), BF16),
                   jax.ShapeDtypeStruct((b, width // LANES, LANES, LANES), F32)],
        scratch_shapes=[pltpu.VMEM((ng, LANES, LANES), F32), pltpu.VMEM((1, wblk), F32),
                        pltpu.VMEM((1, wblk), F32), pltpu.VMEM((1, wblk), F32),
                        pltpu.VMEM((1, 2 * LANES), F32)],
        compiler_params=_params("parallel", "parallel", "arbitrary"),
        name="rwkv",
    )(p, p, p, p, shift0, shift0, shift0, shift0, mu, mu, mu, mu, *vecs, *loras, s0_bd)


def _merge_ffn_kernel(x_ref, a_ref, b_ref, wo_ref, g_ref, wg_ref, wu_ref, wd_ref, o_ref, hn_ref):
    @pl.when(pl.program_id(1) == 0)
    def _():
        half = a_ref.shape[1]
        h = x_ref[...] + (_dg(a_ref[...], wo_ref[:half, :]) + _dg(b_ref[...], wo_ref[half:, :]))
        ms = jnp.mean(h * h, axis=-1, keepdims=True)
        hn_ref[...] = (h * lax.rsqrt(ms + RMS_EPS) * g_ref[...]).astype(BF16)
        o_ref[...] = h

    hn = hn_ref[...]
    gt = _dg(hn, wg_ref[...])
    up = _dg(hn, wu_ref[...])
    act = gt * _sigmoid(gt) * up
    o_ref[...] += _dg(act.astype(BF16), wd_ref[...])


def _merge_ffn(x2d, a, b, wo, g, wg, wu, wd, *, tm, tf):
    n, d = x2d.shape
    half = a.shape[1]
    dff = wg.shape[1]
    return pl.pallas_call(
        _merge_ffn_kernel,
        grid=(n // tm, dff // tf),
        in_specs=[pl.BlockSpec((tm, d), lambda i, j: (i, 0)),
                  pl.BlockSpec((tm, half), lambda i, j: (i, 0)),
                  pl.BlockSpec((tm, half), lambda i, j: (i, 0)),
                  pl.BlockSpec((d, d), lambda i, j: (0, 0), pipeline_mode=pl.Buffered(1)),
                  pl.BlockSpec((1, d), lambda i, j: (0, 0)),
                  pl.BlockSpec((d, tf), lambda i, j: (0, j)),
                  pl.BlockSpec((d, tf), lambda i, j: (0, j)),
                  pl.BlockSpec((tf, d), lambda i, j: (j, 0))],
        out_specs=pl.BlockSpec((tm, d), lambda i, j: (i, 0)),
        out_shape=jax.ShapeDtypeStruct((n, d), F32),
        scratch_shapes=[pltpu.VMEM((tm, d), BF16)],
        compiler_params=_params("parallel", "arbitrary"),
        name="merge_ffn",
    )(x2d, a, b, wo, g.reshape(1, d), wg, wu, wd)


def _pick(n, pref):
    return pref if n % pref == 0 else n


def _block_diag_state(s):
    b, h = s.shape[:2]
    s = s.reshape(b, h // 2, 2, HEAD_B, HEAD_B)
    z = jnp.zeros_like(s[:, :, 0])
    top = jnp.concatenate([s[:, :, 0], z], axis=-1)
    bot = jnp.concatenate([z, s[:, :, 1]], axis=-1)
    return jnp.concatenate([top, bot], axis=-2)


def _unblock_state(s_bd):
    b, g = s_bd.shape[:2]
    s = jnp.stack([s_bd[:, :, :HEAD_B, :HEAD_B], s_bd[:, :, HEAD_B:, HEAD_B:]], axis=2)
    return s.reshape(b, 2 * g, HEAD_B, HEAD_B)


def _layer(x, pos, past, s0, shift0, lam_init, wts):
    b, t, d = x.shape
    n = b * t
    wa = wts["width_a"]
    wb = wts["width_b"]
    x2d = x.reshape(n, d)
    tm = _pick(n, 512)
    tm_in = _pick(n, 1024)
    rope = _rope_tables(pos, max(t, tm_in))
    lams = wts["lams"]

    q, k32, k16, v32, v16, p_rwkv = _inproj(
        x2d, wts["norm1_g"], wts["w_in"], wts["q_norm_g"], wts["k_norm_g"], rope,
        width_a=wa, q_scale=HALF_A ** -0.5 * LOG2E, tm=tm_in, tn=INPROJ_TN)

    q = q.reshape(b, t, wa)
    k16 = k16.reshape(b, t, wa)
    v16 = v16.reshape(b, t, wa)
    if past is None:
        bound = (8.0 * 1.01 * LOG2E) * jnp.max(jnp.abs(wts["q_norm_g"])) * jnp.max(jnp.abs(wts["k_norm_g"]))
        attn = _attn_prompt(q, k16, v16, lams, wts["subln_g"], lam_init, bound,
                            tq=_pick(t, 1024), tk=_pick(t, 2048))
        tb, ng = _pick(t, 256), 4
    else:
        attn = _attn_sample(q, *past, k16, v16, lams, wts["subln_g"], lam_init)
        tb, ng = _pick(t, CHUNK_B), wb // LANES

    p_rwkv = p_rwkv.reshape(b, t, -1)
    rw, s_bd = _rwkv(p_rwkv, shift0, s0, wts["mu"], wts["vecs"], wts["loras"], tb=tb, ng=ng)

    y = _merge_ffn(x2d, attn.reshape(n, wa), rw.reshape(n, wb), wts["w_out"], wts["norm2_g"],
                   wts["w_gate"], wts["w_up"], wts["w_down"], tm=tm, tf=512)

    nh_a = wa // HEAD_A
    rcols = wts["rwkv_cols"]
    return (y.reshape(b, t, d), k32.reshape(b, t, nh_a, HEAD_A), v32.reshape(b, t, nh_a, HEAD_A),
            _unblock_state(s_bd), p_rwkv[:, t - 1:, :rcols])


def kernel(x_prompt, x_sample, cache_attn_k, cache_attn_v, state_rwkv, state_rwkv_shift, norm1_g, w_in, q_norm_g, k_norm_g, lambda_q1, lambda_k1, lambda_q2, lambda_k2, subln_g, mu_rwkv, w0, w2, a0, a2, g2, k_k, k_a, r_k, lnx_g, lnx_b, w_out, norm2_g, w_gate, w_up, w_down):
    depth = w_in.shape[0]
    bp, tp, d = x_prompt.shape
    bs, ts, _ = x_sample.shape
    past_len = cache_attn_k.shape[2]
    nh_a = cache_attn_k.shape[3]
    wa = nh_a * HEAD_A
    wb = d - wa
    rcols = 3 * wb + 3 * LORA
    pad = -rcols % INPROJ_TN
    pos_p = jnp.arange(tp, dtype=jnp.int32)
    pos_s = past_len + jnp.arange(ts, dtype=jnp.int32)

    xp, xs = x_prompt, x_sample
    outs = [[] for _ in range(8)]
    for l in range(depth):
        lam_init = 0.8 - 0.6 * math.exp(-0.3 * l)
        wl = w_in[l].astype(BF16)

        def lora_pad(w, slot):
            return jnp.pad(w, ((slot * LORA, 2 * LANES - (slot + 1) * LORA), (0, 0)))

        row = lambda v: v.reshape(1, -1)
        wts = dict(
            width_a=wa, width_b=wb, rwkv_cols=rcols,
            norm1_g=norm1_g[l], norm2_g=norm2_g[l],
            w_in=jnp.pad(wl, ((0, 0), (0, pad))),
            q_norm_g=row(q_norm_g[l]), k_norm_g=row(k_norm_g[l]),
            lams=[row(lambda_q1[l]), row(lambda_k1[l]), row(lambda_q2[l]), row(lambda_k2[l])],
            subln_g=row(subln_g[l]),
            mu=jnp.pad(row(mu_rwkv[l]), ((0, 0), (0, pad))),
            vecs=[row(w0[l]), row(a0[l]), row(k_k[l]), row(k_a[l]), row(r_k[l]), row(lnx_g[l]), row(lnx_b[l])],
            loras=[lora_pad(w2[l], 0).astype(BF16), lora_pad(a2[l], 1).astype(BF16),
                   lora_pad(g2[l], 2).astype(BF16)],
            w_out=w_out[l].astype(BF16), w_gate=w_gate[l].astype(BF16),
            w_up=w_up[l].astype(BF16), w_down=w_down[l].astype(BF16),
        )
        s0_p = jnp.zeros((bp, wb // LANES, LANES, LANES), F32)
        shift0_p = jnp.zeros((bp, 1, rcols + pad), F32)
        xp, kp, vp, sp, shp = _layer(xp, pos_p, None, s0_p, shift0_p, lam_init, wts)
        past = (cache_attn_k, cache_attn_v, l)
        shift0_s = jnp.pad(state_rwkv_shift[l], ((0, 0), (0, 0), (0, pad)))
        xs, ks, vs, ss, shs = _layer(xs, pos_s, past, _block_diag_state(state_rwkv[l]), shift0_s,
                                     lam_init, wts)
        for lst, val in zip(outs, (kp, vp, sp, shp, ks, vs, ss, shs)):
            lst.append(val)
    return (xp, xs) + tuple(jnp.stack(o) for o in outs)
```

```python
import functools
import math

import jax
import jax.numpy as jnp
from jax import lax
from jax.experimental import pallas as pl
from jax.experimental.pallas import tpu as pltpu

F32 = jnp.float32
BF16 = jnp.bfloat16

LANES = 128
HEAD_A = 128
HALF_A = 64
ROT_HALF = 8
ROPE_THETA = 500000.0
HEAD_B = 64
CHUNK_MASK = 64
CHUNK_B = 64
LORA = 64
RMS_EPS = 1e-6
GN_EPS = 64e-5
NEG_BIG = -1e30
LOG2E = 1.4426950408889634
MAX_UNSHIFTED_LOGIT2 = 64.0
VMEM_LIMIT = 56 * 1024 * 1024
INPROJ_TN = 512
SAFE_TQ = 256

NN = ((1,), (0,))
NT = ((1,), (1,))


def _dg(a, b, dims=NN):
    return lax.dot_general(a, b, (dims, ((), ())), preferred_element_type=F32)


def _split(a):
    hi = a.astype(BF16)
    lo = (a - hi.astype(F32)).astype(BF16)
    return hi, lo


def _mm2(a, b_exact):
    ah, al = _split(a)
    return _dg(ah, b_exact) + _dg(al, b_exact)


def _group_matrix(n, group, value):
    r = lax.broadcasted_iota(jnp.int32, (n, n), 0) // group
    c = lax.broadcasted_iota(jnp.int32, (n, n), 1) // group
    return jnp.where(r == c, value, 0.0).astype(BF16)


def _sigmoid(x):
    return 1.0 / (1.0 + jnp.exp(-x))


def _params(*sem):
    return pltpu.CompilerParams(dimension_semantics=sem, vmem_limit_bytes=VMEM_LIMIT)


def _inproj_kernel(x_ref, g_ref, w_ref, qg_ref, kg_ref, cos_ref, sin_ref,
                   q_ref, k32_ref, k16_ref, v32_ref, v16_ref, p_ref, xn_ref, *, nsec, q_scale):
    j = pl.program_id(1)

    @pl.when(j == 0)
    def _():
        x = x_ref[...]
        ms = jnp.mean(x * x, axis=-1, keepdims=True)
        xn_ref[...] = (x * lax.rsqrt(ms + RMS_EPS) * g_ref[...]).astype(BF16)

    acc = _dg(xn_ref[...], w_ref[...])

    @pl.when(j < 2 * nsec)
    def _():
        is_q = j < nsec
        seg = _group_matrix(LANES, HALF_A, 1.0 / HALF_A)
        hg = jnp.where(is_q, qg_ref[...] * q_scale, kg_ref[...])
        cos, sin = cos_ref[...], sin_ref[...]
        lane = lax.broadcasted_iota(jnp.int32, sin.shape, 1) % HALF_A
        sa = jnp.where(lane < ROT_HALF, sin, 0.0)
        sb = jnp.where(lane < ROT_HALF, 0.0, sin)
        groups = []
        for g in range(acc.shape[1] // LANES):
            xg = acc[:, g * LANES:(g + 1) * LANES]
            ms = _mm2(xg * xg, seg)
            xg = xg * lax.rsqrt(ms + RMS_EPS) * hg
            up = pltpu.roll(xg, LANES - ROT_HALF, axis=1)
            dn = pltpu.roll(xg, ROT_HALF, axis=1)
            groups.append(xg * cos + up * sa + dn * sb)
        rot = jnp.concatenate(groups, axis=1)

        @pl.when(is_q)
        def _():
            q_ref[...] = rot.astype(BF16)

        @pl.when(jnp.logical_not(is_q))
        def _():
            k32_ref[...] = rot
            k16_ref[...] = rot.astype(BF16)

    @pl.when(jnp.logical_and(j >= 2 * nsec, j < 3 * nsec))
    def _():
        v32_ref[...] = acc
        v16_ref[...] = acc.astype(BF16)

    @pl.when(j >= 3 * nsec)
    def _():
        p_ref[...] = acc


def _inproj(x2d, g, w, qg, kg, rope, *, width_a, q_scale, tm, tn):
    n, d = x2d.shape
    ncol = w.shape[1]
    nsec = width_a // tn
    nblk = rope[0].shape[0] // tm
    sec = lambda k: (lambda i, j: (i, jnp.clip(j - k * nsec, 0, nsec - 1)))
    vec = pl.BlockSpec((1, LANES), lambda i, j: (0, 0))
    tab = pl.BlockSpec((tm, LANES), lambda i, j: (i % nblk, 0))
    blk = lambda imap: pl.BlockSpec((tm, tn), imap)
    return pl.pallas_call(
        functools.partial(_inproj_kernel, nsec=nsec, q_scale=q_scale),
        grid=(n // tm, ncol // tn),
        in_specs=[pl.BlockSpec((tm, d), lambda i, j: (i, 0)),
                  pl.BlockSpec((1, d), lambda i, j: (0, 0)),
                  pl.BlockSpec((d, tn), lambda i, j: (0, j)),
                  vec, vec, tab, tab],
        out_specs=[blk(sec(0)), blk(sec(1)), blk(sec(1)), blk(sec(2)), blk(sec(2)),
                   blk(lambda i, j: (i, jnp.maximum(j - 3 * nsec, 0)))],
        out_shape=[jax.ShapeDtypeStruct((n, width_a), BF16),
                   jax.ShapeDtypeStruct((n, width_a), F32), jax.ShapeDtypeStruct((n, width_a), BF16),
                   jax.ShapeDtypeStruct((n, width_a), F32), jax.ShapeDtypeStruct((n, width_a), BF16),
                   jax.ShapeDtypeStruct((n, ncol - 3 * width_a), F32)],
        scratch_shapes=[pltpu.VMEM((tm, d), BF16)],
        compiler_params=_params("parallel", "arbitrary"),
        name="inproj",
    )(x2d, g.reshape(1, d), w, qg, kg, *rope)


def _rope_tables(pos, rows):
    inv = jnp.float32(ROPE_THETA) ** (-jnp.arange(0, 2 * ROT_HALF, 2, dtype=F32) / (2 * ROT_HALF))
    lane = jnp.arange(HEAD_A) % HALF_A
    inv_lane = jnp.where(lane < 2 * ROT_HALF, inv[lane % ROT_HALF], 0.0)
    sign = jnp.where(lane < ROT_HALF, -1.0, 1.0)
    ang = pos.astype(F32)[:, None] * inv_lane[None, :]
    tabs = [jnp.cos(ang), jnp.sin(ang) * sign]
    t = pos.shape[0]
    if rows > t:
        tabs = [jnp.tile(h, (rows // t, 1)) for h in tabs]
    return tabs


def _stack_halves(q):
    lane = lax.broadcasted_iota(jnp.int32, q.shape, 1)
    zero = jnp.zeros_like(q)
    return jnp.concatenate([jnp.where(lane < HALF_A, q, zero), jnp.where(lane < HALF_A, zero, q)], axis=0)


def _lambda(lq1, lk1, lq2, lk2, lam_init):
    return (jnp.exp(jnp.sum(lq1 * lk1, axis=-1, keepdims=True))
            - jnp.exp(jnp.sum(lq2 * lk2, axis=-1, keepdims=True)) + lam_init)


def _attn_finish(num1, den1, num2, den2, lam, g, lam_init):
    o = num1 / den1 - lam * (num2 / den2)
    ms = jnp.mean(o * o, axis=-1, keepdims=True)
    return o * lax.rsqrt(ms + RMS_EPS) * (g * (1.0 - lam_init))


def _chunk_mask(shape, tq):
    row = lax.broadcasted_iota(jnp.int32, shape, 0)
    col = lax.broadcasted_iota(jnp.int32, shape, 1)
    qc = jnp.where(row >= tq, row - tq, row) // CHUNK_MASK
    return col // CHUNK_MASK <= qc


def _attn_prompt_kernel(q_ref, k_ref, v_ref, lq1, lk1, lq2, lk2, g_ref, o_ref, acc_ref, *, tq, tk, lam_init):
    i = pl.program_id(2)
    qs = _stack_halves(q_ref[...])
    nfull = (i * tq) // tk
    nrest = (i * tq - nfull * tk) // tq

    half = tq // 2

    def tile(start, width, diagonal=False, lhs=qs):
        rows = pl.ds(pl.multiple_of(start, half), width)
        p = jnp.exp2(_dg(lhs, k_ref[rows, :], NT))
        if diagonal:
            p = jnp.where(_chunk_mask(p.shape, lhs.shape[0] // 2), p, 0.0)
        ve = jnp.concatenate([v_ref[rows, :], jnp.ones((width, HEAD_A), BF16)], axis=1)
        return _dg(p.astype(BF16), ve)

    acc_ref[...] = tile(i * tq, half, True)
    late = tile(i * tq + half, half, True, jnp.concatenate([qs[half:tq], qs[tq + half:]], axis=0))
    acc_ref[half:tq, :] += late[:half]
    acc_ref[tq + half:, :] += late[half:]

    def rest(j, c):
        acc_ref[...] += tile(nfull * tk + j * tq, tq, False)
        return c

    def full(j, c):
        acc_ref[...] += tile(j * tk, tk, False)
        return c

    lax.fori_loop(0, nrest, rest, 0)
    lax.fori_loop(0, nfull, full, 0)
    acc = acc_ref[...]
    lam = _lambda(lq1[...], lk1[...], lq2[...], lk2[...], lam_init)
    o = _attn_finish(acc[:tq, :HEAD_A], acc[:tq, HEAD_A:], acc[tq:, :HEAD_A], acc[tq:, HEAD_A:],
                     lam, g_ref[...], lam_init)
    o_ref[...] = o.astype(o_ref.dtype)


def _attn_prompt_safe_kernel(q_ref, k_ref, v_ref, lq1, lk1, lq2, lk2, g_ref, o_ref, *, tq, lam_init):
    i = pl.program_id(2)
    qs = _stack_halves(q_ref[...])

    def tile(j, carry, masked):
        m, l, acc = carry
        start = pl.multiple_of(j * tq, tq)
        k = k_ref[pl.ds(start, tq), :]
        v = v_ref[pl.ds(start, tq), :]
        s = _dg(qs, k, NT)
        if masked:
            s = jnp.where(_chunk_mask(s.shape, tq), s, NEG_BIG)
        m_new = jnp.maximum(m, jnp.max(s, axis=-1, keepdims=True))
        p = jnp.exp2(s - m_new)
        alpha = jnp.exp2(m - m_new)
        l = alpha * l + jnp.sum(p, axis=-1, keepdims=True)
        acc = alpha * acc + _dg(p.astype(BF16), v)
        return m_new, l, acc

    init = (jnp.full((2 * tq, 1), NEG_BIG, F32), jnp.zeros((2 * tq, 1), F32),
            jnp.zeros((2 * tq, HEAD_A), F32))
    carry = lax.fori_loop(0, i, lambda j, c: tile(j, c, False), init)
    _, l, acc = tile(i, carry, True)
    lam = _lambda(lq1[...], lk1[...], lq2[...], lk2[...], lam_init)
    o = _attn_finish(acc[:tq], l[:tq], acc[tq:], l[tq:], lam, g_ref[...], lam_init)
    o_ref[...] = o.astype(o_ref.dtype)


def _attn_prompt(q, k, v, lams, g, lam_init, logit_bound, *, tq, tk):
    b, t, width = q.shape
    nh = width // HEAD_A
    small = pl.BlockSpec((1, HALF_A), lambda bi, h, i: (0, 0))

    def call(body, tq, scratch, name, **kw):
        return pl.pallas_call(
            functools.partial(body, tq=tq, lam_init=lam_init, **kw),
            grid=(b, nh, t // tq),
            in_specs=[
                pl.BlockSpec((None, tq, HEAD_A), lambda bi, h, i: (bi, i, h)),
                pl.BlockSpec((None, t, HEAD_A), lambda bi, h, i: (bi, 0, h)),
                pl.BlockSpec((None, t, HEAD_A), lambda bi, h, i: (bi, 0, h)),
                small, small, small, small,
                pl.BlockSpec((1, HEAD_A), lambda bi, h, i: (0, 0)),
            ],
            out_specs=pl.BlockSpec((None, tq, HEAD_A), lambda bi, h, i: (bi, i, h)),
            out_shape=jax.ShapeDtypeStruct((b, t, width), BF16),
            scratch_shapes=scratch,
            compiler_params=_params("parallel", "parallel", "arbitrary"),
            name=name,
        )

    fast = call(_attn_prompt_kernel, tq, [pltpu.VMEM((2 * tq, 2 * HEAD_A), F32)], "attn_prompt", tk=tk)
    safe = call(_attn_prompt_safe_kernel, _pick(t, SAFE_TQ), [], "attn_prompt_safe")
    return lax.cond(logit_bound <= MAX_UNSHIFTED_LOGIT2, fast, safe, q, k, v, *lams, g)


def _attn_sample_kernel(q_ref, kp_ref, vp_ref, kn_ref, vn_ref, lq1, lk1, lq2, lk2, g_ref, o_ref,
                        *, tq, past, nh, lam_init):
    lam = _lambda(lq1[...], lk1[...], lq2[...], lk2[...], lam_init)
    for h in range(nh):
        cols = slice(h * HEAD_A, (h + 1) * HEAD_A)
        head_rows = pl.ds(h, past, stride=nh)
        qs = _stack_halves(q_ref[:, cols])
        s_p = _dg(qs, kp_ref[head_rows, :].astype(BF16), NT)
        s_n = _dg(qs, kn_ref[:, cols], NT)
        row = lax.broadcasted_iota(jnp.int32, s_n.shape, 0)
        col = lax.broadcasted_iota(jnp.int32, s_n.shape, 1)
        qc = (past + jnp.where(row >= tq, row - tq, row)) // CHUNK_MASK
        s_n = jnp.where((past + col) // CHUNK_MASK <= qc, s_n, NEG_BIG)
        m = jnp.maximum(jnp.max(s_p, axis=-1, keepdims=True), jnp.max(s_n, axis=-1, keepdims=True))
        p_p = jnp.exp2(s_p - m)
        p_n = jnp.exp2(s_n - m)
        l = jnp.sum(p_p, axis=-1, keepdims=True) + jnp.sum(p_n, axis=-1, keepdims=True)
        acc = (_dg(p_p.astype(BF16), vp_ref[head_rows, :].astype(BF16))
               + _dg(p_n.astype(BF16), vn_ref[:, cols]))
        o = _attn_finish(acc[:tq], l[:tq], acc[tq:], l[tq:], lam, g_ref[...], lam_init)
        o_ref[:, cols] = o.astype(o_ref.dtype)


def _attn_sample(q, k_past, v_past, layer, k_new, v_new, lams, g, lam_init):
    b, t, width = q.shape
    depth, _, past, nh, _ = k_past.shape
    k_past = k_past.reshape(depth, b, past * nh, HEAD_A)
    v_past = v_past.reshape(depth, b, past * nh, HEAD_A)
    small = pl.BlockSpec((1, HALF_A), lambda bi: (0, 0))
    new = pl.BlockSpec((None, t, width), lambda bi: (bi, 0, 0))
    old = pl.BlockSpec((None, None, past * nh, HEAD_A), lambda bi: (layer, bi, 0, 0))
    return pl.pallas_call(
        functools.partial(_attn_sample_kernel, tq=t, past=past, nh=nh, lam_init=lam_init),
        grid=(b,),
        in_specs=[new, old, old, new, new, small, small, small, small,
                  pl.BlockSpec((1, HEAD_A), lambda bi: (0, 0))],
        out_specs=new,
        out_shape=jax.ShapeDtypeStruct((b, t, width), BF16),
        compiler_params=_params("parallel"),
        name="attn_sample",
    )(q, k_past, v_past, k_new, v_new, *lams, g)


def _stack_heads(x):
    lane = lax.broadcasted_iota(jnp.int32, x.shape, 1)
    return jnp.concatenate([jnp.where(lane < HEAD_B, x, 0.0), jnp.where(lane < HEAD_B, 0.0, x)], axis=0)


def _rwkv_kernel(pr_ref, pk_ref, pv_ref, pl_ref, sr_ref, sk_ref, sv_ref, sl_ref,
                 mr_ref, mk_ref, mv_ref, ml_ref,
                 w0_ref, a0_ref, kk_ref, ka_ref, rk_ref, lg_ref, lb_ref,
                 w2_ref, a2_ref, g2_ref, s0_ref,
                 out_ref, sout_ref,
                 s_sc, cr_sc, ck_sc, cv_sc, cl_sc, *, tb, ng):
    t = pl.program_id(2)
    width = ng * LANES

    @pl.when(t == 0)
    def _():
        s_sc[...] = s0_ref[...]
        cr_sc[...] = sr_ref[...]
        ck_sc[...] = sk_ref[...]
        cv_sc[...] = sv_ref[...]
        cl_sc[...] = sl_ref[...]

    def shifted(p_ref, carry_ref, mu_ref):
        p = p_ref[...]
        prev = pltpu.roll(p, 1, axis=0)
        row = lax.broadcasted_iota(jnp.int32, p.shape, 0)
        prev = jnp.where(row == 0, carry_ref[...], prev)
        carry_ref[...] = p[tb - 1:tb, :]
        return p + (prev - p) * mu_ref[...]

    xr = shifted(pr_ref, cr_sc, mr_ref)
    xk = shifted(pk_ref, ck_sc, mk_ref)
    xv = shifted(pv_ref, cv_sc, mv_ref)
    xl = shifted(pl_ref, cl_sc, ml_ref)

    slab = min(width, 2 * LANES)

    def head_sums(x, value):
        mat = _group_matrix(slab, HEAD_B, value)
        parts = [_mm2(x[:, o:o + slab], mat) for o in range(0, width, slab)]
        return parts[0] if len(parts) == 1 else jnp.concatenate(parts, axis=1)

    z = w0_ref[...] + _dg(jnp.tanh(xl).astype(BF16), w2_ref[...])
    logw = -math.exp(-0.5) * _sigmoid(z)
    ag = _sigmoid(a0_ref[...] + _dg(xl.astype(BF16), a2_ref[...]))
    gate = _dg(_sigmoid(xl).astype(BF16), g2_ref[...])
    kk = xk * kk_ref[...]
    kk = kk * lax.rsqrt(jnp.maximum(head_sums(kk * kk, 1.0), 1e-24))
    km = xk * (1.0 + (ag - 1.0) * ka_ref[...])
    bonus = head_sums(xr * km * rk_ref[...], 1.0) * xv

    c = CHUNK_B
    c2 = 2 * c
    ri = lax.broadcasted_iota(jnp.int32, (c, c), 0)
    ci = lax.broadcasted_iota(jnp.int32, (c, c), 1)
    tri = jnp.where(ci <= ri, 1.0, 0.0).astype(BF16)
    r2 = lax.broadcasted_iota(jnp.int32, (c2, c2), 0)
    c2i = lax.broadcasted_iota(jnp.int32, (c2, c2), 1)
    rm = r2 % c
    cm = c2i % c
    strict = cm < rm
    incl = cm <= rm
    eye = jnp.where(r2 == c2i, 1.0, 0.0)

    def off_blocks(s):
        rs = rm // s
        return (rs - cm // s) * 2 + rs % 2 == 3

    nchunk = tb // c
    zero = jnp.zeros((c2, c2), BF16)
    pre = {}

    def prepare(keys):
        for g, n in keys:
            rows = slice(n * c, (n + 1) * c)
            cols = slice(g * LANES, (g + 1) * LANES)
            lw = logw[rows, cols]
            h1 = lw.astype(BF16)
            r1 = lw - h1.astype(F32)
            h2 = r1.astype(BF16)
            h3 = (r1 - h2.astype(F32)).astype(BF16)
            cum = _dg(tri, h1) + (_dg(tri, h2) + _dg(tri, h3))
            w_in = jnp.exp(cum)
            w_ex = jnp.exp(cum - lw)
            w_inv = jnp.exp(-cum)
            kc = kk[rows, cols]
            ar = jnp.concatenate([_stack_heads(-kc * w_ex),
                                  _stack_heads(xr[rows, cols] * w_in)], axis=0).astype(BF16)
            bk = jnp.concatenate([_stack_heads(kc * ag[rows, cols] * w_inv),
                                  _stack_heads(km[rows, cols] * w_inv)], axis=0).astype(BF16)
            v_s = _stack_heads(xv[rows, cols])
            pre[g, n] = dict(ar=ar, bk=bk, v_h=v_s.astype(BF16), vt_h=v_s.T.astype(BF16),
                             w_last=w_in[c - 1:c, :])

        for key in keys:
            d = pre[key]
            gm = _dg(d["ar"], d["bk"], NT)
            d["n_ab"] = jnp.where(strict, gm[:c2, :c2], 0.0).astype(BF16)
            d["n_rb"] = jnp.where(incl, gm[c2:, :c2], 0.0).astype(BF16)
            n_k = jnp.concatenate([jnp.where(strict, gm[:c2, c2:], 0.0),
                                   jnp.where(incl, gm[c2:, c2:], 0.0)], axis=0).astype(BF16)
            kv = _dg(n_k, d["v_h"])
            d["akv"] = kv[:c2].astype(BF16)
            d["yv"] = kv[c2:]
            d["vk"] = _dg(d["vt_h"], d["bk"][c2:])

        for key in keys:
            d = pre[key]
            d["tinv"] = eye + jnp.where(off_blocks(1), d["n_ab"], zero).astype(F32)
        s = 2
        while s < c:
            sel = off_blocks(s)
            for key in keys:
                d = pre[key]
                d["tinv_h"] = d["tinv"].astype(BF16)
                d["tmp"] = _dg(jnp.where(sel, d["n_ab"], zero), d["tinv_h"]).astype(BF16)
            for key in keys:
                d = pre[key]
                d["tinv"] = d["tinv"] + _dg(d["tinv_h"], d["tmp"])
            s *= 2

        for key in keys:
            d = pre[key]
            tx = _dg(d["tinv"].astype(BF16), jnp.concatenate([d["ar"][:c2], d["akv"]], axis=1))
            d["tar"] = jnp.concatenate([tx[:, :c2].astype(BF16), d["ar"][c2:]], axis=0)
            d["u0"] = tx[:, c2:]
            d["u0t"] = tx[:, c2:].T

    state = [s_sc[g] for g in range(ng)]
    ys = {}

    def advance(n):
        for g in range(ng):
            d = pre[g, n]
            s_h = state[g].astype(BF16)
            d["ut"] = _dg(s_h, d["tar"][:c2], NT) + d["u0t"]
            d["us"] = _dg(d["tar"], s_h, NT)
        for g in range(ng):
            d = pre[g, n]
            u1 = d["us"][:c2] + d["u0"]
            y_s = d["us"][c2:] + _dg(d["n_rb"], u1.astype(BF16)) + d["yv"]
            ys[g, n] = y_s[:c] + y_s[c:]
            state[g] = (state[g] + _dg(d["ut"].astype(BF16), d["bk"][:c2]) + d["vk"]) * d["w_last"]

    prepare([(g, n) for n in range(nchunk) for g in range(ng)])
    for n in range(nchunk):
        advance(n)
    for g in range(ng):
        s_sc[g] = state[g]

    @pl.when(t == pl.num_programs(2) - 1)
    def _():
        for g in range(ng):
            sout_ref[g] = state[g]

    y = jnp.concatenate(
        [jnp.concatenate([ys[g, n] for n in range(nchunk)], axis=0) for g in range(ng)], axis=1)
    mean = head_sums(y, 1.0 / HEAD_B)
    dlt = y - mean
    var = head_sums(dlt * dlt, 1.0 / HEAD_B)
    yn = dlt * lax.rsqrt(var + GN_EPS) * lg_ref[...] + lb_ref[...]
    out_ref[...] = ((yn + bonus) * gate).astype(out_ref.dtype)


def _rwkv(p, shift0, s0_bd, mu, vecs, loras, *, tb, ng):
    b, t, _ = p.shape
    width = vecs[0].shape[-1]
    wblk = ng * LANES
    nblk = width // wblk
    lcol = 3 * width // (2 * LANES)

    def col(off):
        return lambda bi, g, ti: (bi, ti, off + g)

    def row(off):
        return lambda bi, g, ti: (bi, 0, off + g)

    def par(off):
        return lambda bi, g, ti: (0, off + g)

    in_specs = (
        [pl.BlockSpec((None, tb, wblk), col(k * nblk)) for k in range(3)]
        + [pl.BlockSpec((None, tb, 2 * LANES), lambda bi, g, ti: (bi, ti, lcol))]
        + [pl.BlockSpec((None, 1, wblk), row(k * nblk)) for k in range(3)]
        + [pl.BlockSpec((None, 1, 2 * LANES), lambda bi, g, ti: (bi, 0, lcol))]
        + [pl.BlockSpec((1, wblk), par(k * nblk)) for k in range(3)]
        + [pl.BlockSpec((1, 2 * LANES), lambda bi, g, ti: (0, lcol))]
        + [pl.BlockSpec((1, wblk), par(0)) for _ in vecs]
        + [pl.BlockSpec((2 * LANES, wblk), par(0)) for _ in loras]
        + [pl.BlockSpec((None, ng, LANES, LANES), lambda bi, g, ti: (bi, g, 0, 0))]
    )
    return pl.pallas_call(
        functools.partial(_rwkv_kernel, tb=tb, ng=ng),
        grid=(b, nblk, t // tb),
        in_specs=in_specs,
        out_specs=[pl.BlockSpec((None, tb, wblk), col(0)),
                   pl.BlockSpec((None, ng, LANES, LANES), lambda bi, g, ti: (bi, g, 0, 0))],
        out_shape=[jax.ShapeDtypeStruct((b, t, width), BF16),
                   jax.ShapeDtypeStruct((b, width // LANES, LANES, LANES), F32)],
        scratch_shapes=[pltpu.VMEM((ng, LANES, LANES), F32), pltpu.VMEM((1, wblk), F32),
                        pltpu.VMEM((1, wblk), F32), pltpu.VMEM((1, wblk), F32),
                        pltpu.VMEM((1, 2 * LANES), F32)],
        compiler_params=_params("parallel", "parallel", "arbitrary"),
        name="rwkv",
    )(p, p, p, p, shift0, shift0, shift0, shift0, mu, mu, mu, mu, *vecs, *loras, s0_bd)


def _merge_ffn_kernel(x_ref, a_ref, b_ref, wo_ref, g_ref, wg_ref, wu_ref, wd_ref, o_ref, hn_ref):
    @pl.when(pl.program_id(1) == 0)
    def _():
        half = a_ref.shape[1]
        h = x_ref[...] + (_dg(a_ref[...], wo_ref[:half, :]) + _dg(b_ref[...], wo_ref[half:, :]))
        ms = jnp.mean(h * h, axis=-1, keepdims=True)
        hn_ref[...] = (h * lax.rsqrt(ms + RMS_EPS) * g_ref[...]).astype(BF16)
        o_ref[...] = h

    hn = hn_ref[...]
    gt = _dg(hn, wg_ref[...])
    up = _dg(hn, wu_ref[...])
    act = gt * _sigmoid(gt) * up
    o_ref[...] += _dg(act.astype(BF16), wd_ref[...])


def _merge_ffn(x2d, a, b, wo, g, wg, wu, wd, *, tm, tf):
    n, d = x2d.shape
    half = a.shape[1]
    dff = wg.shape[1]
    return pl.pallas_call(
        _merge_ffn_kernel,
        grid=(n // tm, dff // tf),
        in_specs=[pl.BlockSpec((tm, d), lambda i, j: (i, 0)),
                  pl.BlockSpec((tm, half), lambda i, j: (i, 0)),
                  pl.BlockSpec((tm, half), lambda i, j: (i, 0)),
                  pl.BlockSpec((d, d), lambda i, j: (0, 0), pipeline_mode=pl.Buffered(1)),
                  pl.BlockSpec((1, d), lambda i, j: (0, 0)),
                  pl.BlockSpec((d, tf), lambda i, j: (0, j)),
                  pl.BlockSpec((d, tf), lambda i, j: (0, j)),
                  pl.BlockSpec((tf, d), lambda i, j: (j, 0))],
        out_specs=pl.BlockSpec((tm, d), lambda i, j: (i, 0)),
        out_shape=jax.ShapeDtypeStruct((n, d), F32),
        scratch_shapes=[pltpu.VMEM((tm, d), BF16)],
        compiler_params=_params("parallel", "arbitrary"),
        name="merge_ffn",
    )(x2d, a, b, wo, g.reshape(1, d), wg, wu, wd)


def _pick(n, pref):
    return pref if n % pref == 0 else n


def _block_diag_state(s):
    b, h = s.shape[:2]
    s = s.reshape(b, h // 2, 2, HEAD_B, HEAD_B)
    z = jnp.zeros_like(s[:, :, 0])
    top = jnp.concatenate([s[:, :, 0], z], axis=-1)
    bot = jnp.concatenate([z, s[:, :, 1]], axis=-1)
    return jnp.concatenate([top, bot], axis=-2)


def _unblock_state(s_bd):
    b, g = s_bd.shape[:2]
    s = jnp.stack([s_bd[:, :, :HEAD_B, :HEAD_B], s_bd[:, :, HEAD_B:, HEAD_B:]], axis=2)
    return s.reshape(b, 2 * g, HEAD_B, HEAD_B)


def _layer(x, pos, past, s0, shift0, lam_init, wts):
    b, t, d = x.shape
    n = b * t
    wa = wts["width_a"]
    wb = wts["width_b"]
    x2d = x.reshape(n, d)
    tm = _pick(n, 512)
    tm_in = _pick(n, 1024)
    rope = _rope_tables(pos, max(t, tm_in))
    lams = wts["lams"]

    q, k32, k16, v32, v16, p_rwkv = _inproj(
        x2d, wts["norm1_g"], wts["w_in"], wts["q_norm_g"], wts["k_norm_g"], rope,
        width_a=wa, q_scale=HALF_A ** -0.5 * LOG2E, tm=tm_in, tn=INPROJ_TN)

    q = q.reshape(b, t, wa)
    k16 = k16.reshape(b, t, wa)
    v16 = v16.reshape(b, t, wa)
    if past is None:
        bound = (8.0 * 1.01 * LOG2E) * jnp.max(jnp.abs(wts["q_norm_g"])) * jnp.max(jnp.abs(wts["k_norm_g"]))
        attn = _attn_prompt(q, k16, v16, lams, wts["subln_g"], lam_init, bound,
                            tq=_pick(t, 1024), tk=_pick(t, 2048))
        tb, ng = _pick(t, 256), 4
    else:
        attn = _attn_sample(q, *past, k16, v16, lams, wts["subln_g"], lam_init)
        tb, ng = _pick(t, CHUNK_B), wb // LANES

    p_rwkv = p_rwkv.reshape(b, t, -1)
    rw, s_bd = _rwkv(p_rwkv, shift0, s0, wts["mu"], wts["vecs"], wts["loras"], tb=tb, ng=ng)

    y = _merge_ffn(x2d, attn.reshape(n, wa), rw.reshape(n, wb), wts["w_out"], wts["norm2_g"],
                   wts["w_gate"], wts["w_up"], wts["w_down"], tm=tm, tf=512)

    nh_a = wa // HEAD_A
    rcols = wts["rwkv_cols"]
    return (y.reshape(b, t, d), k32.reshape(b, t, nh_a, HEAD_A), v32.reshape(b, t, nh_a, HEAD_A),
            _unblock_state(s_bd), p_rwkv[:, t - 1:, :rcols])


def kernel(x_prompt, x_sample, cache_attn_k, cache_attn_v, state_rwkv, state_rwkv_shift, norm1_g, w_in, q_norm_g, k_norm_g, lambda_q1, lambda_k1, lambda_q2, lambda_k2, subln_g, mu_rwkv, w0, w2, a0, a2, g2, k_k, k_a, r_k, lnx_g, lnx_b, w_out, norm2_g, w_gate, w_up, w_down):
    depth = w_in.shape[0]
    bp, tp, d = x_prompt.shape
    bs, ts, _ = x_sample.shape
    past_len = cache_attn_k.shape[2]
    nh_a = cache_attn_k.shape[3]
    wa = nh_a * HEAD_A
    wb = d - wa
    rcols = 3 * wb + 3 * LORA
    pad = -rcols % INPROJ_TN
    pos_p = jnp.arange(tp, dtype=jnp.int32)
    pos_s = past_len + jnp.arange(ts, dtype=jnp.int32)

    xp, xs = x_prompt, x_sample
    outs = [[] for _ in range(8)]
    for l in range(depth):
        lam_init = 0.8 - 0.6 * math.exp(-0.3 * l)

        def lora_pad(w, slot):
            return jnp.pad(w, ((slot * LORA, 2 * LANES - (slot + 1) * LORA), (0, 0)))

        row = lambda v: v.reshape(1, -1)
        wts = dict(
            width_a=wa, width_b=wb, rwkv_cols=rcols,
            norm1_g=norm1_g[l], norm2_g=norm2_g[l],
            w_in=jnp.pad(w_in[l], ((0, 0), (0, pad))).astype(BF16),
            q_norm_g=row(q_norm_g[l]), k_norm_g=row(k_norm_g[l]),
            lams=[row(lambda_q1[l]), row(lambda_k1[l]), row(lambda_q2[l]), row(lambda_k2[l])],
            subln_g=row(subln_g[l]),
            mu=jnp.pad(row(mu_rwkv[l]), ((0, 0), (0, pad))),
            vecs=[row(w0[l]), row(a0[l]), row(k_k[l]), row(k_a[l]), row(r_k[l]), row(lnx_g[l]), row(lnx_b[l])],
            loras=[lora_pad(w2[l], 0).astype(BF16), lora_pad(a2[l], 1).astype(BF16),
                   lora_pad(g2[l], 2).astype(BF16)],
            w_out=w_out[l].astype(BF16), w_gate=w_gate[l].astype(BF16),
            w_up=w_up[l].astype(BF16), w_down=w_down[l].astype(BF16),
        )
        s0_p = jnp.zeros((bp, wb // LANES, LANES, LANES), F32)
        shift0_p = jnp.zeros((bp, 1, rcols + pad), F32)
        xp, kp, vp, sp, shp = _layer(xp, pos_p, None, s0_p, shift0_p, lam_init, wts)
        past = (cache_attn_k, cache_attn_v, l)
        shift0_s = jnp.pad(state_rwkv_shift[l], ((0, 0), (0, 0), (0, pad)))
        xs, ks, vs, ss, shs = _layer(xs, pos_s, past, _block_diag_state(state_rwkv[l]), shift0_s,
                                     lam_init, wts)
        for lst, val in zip(outs, (kp, vp, sp, shp, ks, vs, ss, shs)):
            lst.append(val)
    return (xp, xs) + tuple(jnp.stack(o) for o in outs)
```

```python
import functools
import math

import jax
import jax.numpy as jnp
from jax import lax
from jax.experimental import pallas as pl
from jax.experimental.pallas import tpu as pltpu

F32 = jnp.float32
BF16 = jnp.bfloat16

LANES = 128
HEAD_A = 128
HALF_A = 64
ROT_HALF = 8
ROPE_THETA = 500000.0
HEAD_B = 64
CHUNK_MASK = 64
CHUNK_B = 64
LORA = 64
RMS_EPS = 1e-6
GN_EPS = 64e-5
NEG_BIG = -1e30
LOG2E = 1.4426950408889634
MAX_UNSHIFTED_LOGIT2 = 64.0
VMEM_LIMIT = 56 * 1024 * 1024
INPROJ_TN = 512
SAFE_TQ = 256

NN = ((1,), (0,))
NT = ((1,), (1,))


def _dg(a, b, dims=NN):
    return lax.dot_general(a, b, (dims, ((), ())), preferred_element_type=F32)


def _group_sums(a, group_matrix):
    return _dg(a.astype(BF16), group_matrix)


def _group_matrix(n, group, value):
    r = lax.broadcasted_iota(jnp.int32, (n, n), 0) // group
    c = lax.broadcasted_iota(jnp.int32, (n, n), 1) // group
    return jnp.where(r == c, value, 0.0).astype(BF16)


def _sigmoid(x):
    return 1.0 / (1.0 + jnp.exp(-x))


def _params(*sem):
    return pltpu.CompilerParams(dimension_semantics=sem, vmem_limit_bytes=VMEM_LIMIT)


def _inproj_kernel(x_ref, g_ref, w_ref, qg_ref, kg_ref, cos_ref, sin_ref,
                   q_ref, k32_ref, k16_ref, v32_ref, v16_ref, p_ref, xn_ref, *, nsec, q_scale):
    j = pl.program_id(1)

    @pl.when(j == 0)
    def _():
        x = x_ref[...]
        ms = jnp.mean(x * x, axis=-1, keepdims=True)
        xn_ref[...] = (x * lax.rsqrt(ms + RMS_EPS) * g_ref[...]).astype(BF16)

    acc = _dg(xn_ref[...], w_ref[...])

    @pl.when(j < 2 * nsec)
    def _():
        is_q = j < nsec
        seg = _group_matrix(LANES, HALF_A, 1.0 / HALF_A)
        hg = jnp.where(is_q, qg_ref[...] * q_scale, kg_ref[...])
        cos, sin = cos_ref[...], sin_ref[...]
        lane = lax.broadcasted_iota(jnp.int32, sin.shape, 1) % HALF_A
        sa = jnp.where(lane < ROT_HALF, sin, 0.0)
        sb = jnp.where(lane < ROT_HALF, 0.0, sin)
        groups = []
        for g in range(acc.shape[1] // LANES):
            xg = acc[:, g * LANES:(g + 1) * LANES]
            ms = _group_sums(xg * xg, seg)
            xg = xg * lax.rsqrt(ms + RMS_EPS) * hg
            up = pltpu.roll(xg, LANES - ROT_HALF, axis=1)
            dn = pltpu.roll(xg, ROT_HALF, axis=1)
            groups.append(xg * cos + up * sa + dn * sb)
        rot = jnp.concatenate(groups, axis=1)

        @pl.when(is_q)
        def _():
            q_ref[...] = rot.astype(BF16)

        @pl.when(jnp.logical_not(is_q))
        def _():
            k32_ref[...] = rot
            k16_ref[...] = rot.astype(BF16)

    @pl.when(jnp.logical_and(j >= 2 * nsec, j < 3 * nsec))
    def _():
        v32_ref[...] = acc
        v16_ref[...] = acc.astype(BF16)

    @pl.when(j >= 3 * nsec)
    def _():
        p_ref[...] = acc


def _inproj(x2d, g, w, qg, kg, rope, *, width_a, q_scale, tm, tn):
    n, d = x2d.shape
    ncol = w.shape[1]
    nsec = width_a // tn
    nblk = rope[0].shape[0] // tm
    sec = lambda k: (lambda i, j: (i, jnp.clip(j - k * nsec, 0, nsec - 1)))
    vec = pl.BlockSpec((1, LANES), lambda i, j: (0, 0))
    tab = pl.BlockSpec((tm, LANES), lambda i, j: (i % nblk, 0))
    blk = lambda imap: pl.BlockSpec((tm, tn), imap)
    return pl.pallas_call(
        functools.partial(_inproj_kernel, nsec=nsec, q_scale=q_scale),
        grid=(n // tm, ncol // tn),
        in_specs=[pl.BlockSpec((tm, d), lambda i, j: (i, 0)),
                  pl.BlockSpec((1, d), lambda i, j: (0, 0)),
                  pl.BlockSpec((d, tn), lambda i, j: (0, j)),
                  vec, vec, tab, tab],
        out_specs=[blk(sec(0)), blk(sec(1)), blk(sec(1)), blk(sec(2)), blk(sec(2)),
                   blk(lambda i, j: (i, jnp.maximum(j - 3 * nsec, 0)))],
        out_shape=[jax.ShapeDtypeStruct((n, width_a), BF16),
                   jax.ShapeDtypeStruct((n, width_a), F32), jax.ShapeDtypeStruct((n, width_a), BF16),
                   jax.ShapeDtypeStruct((n, width_a), F32), jax.ShapeDtypeStruct((n, width_a), BF16),
                   jax.ShapeDtypeStruct((n, ncol - 3 * width_a), F32)],
        scratch_shapes=[pltpu.VMEM((tm, d), BF16)],
        compiler_params=_params("parallel", "arbitrary"),
        name="inproj",
    )(x2d, g.reshape(1, d), w, qg, kg, *rope)


def _rope_tables(pos, rows):
    inv = jnp.float32(ROPE_THETA) ** (-jnp.arange(0, 2 * ROT_HALF, 2, dtype=F32) / (2 * ROT_HALF))
    lane = jnp.arange(HEAD_A) % HALF_A
    inv_lane = jnp.where(lane < 2 * ROT_HALF, inv[lane % ROT_HALF], 0.0)
    sign = jnp.where(lane < ROT_HALF, -1.0, 1.0)
    ang = pos.astype(F32)[:, None] * inv_lane[None, :]
    tabs = [jnp.cos(ang), jnp.sin(ang) * sign]
    t = pos.shape[0]
    if rows > t:
        tabs = [jnp.tile(h, (rows // t, 1)) for h in tabs]
    return tabs


def _stack_halves(q):
    lane = lax.broadcasted_iota(jnp.int32, q.shape, 1)
    zero = jnp.zeros_like(q)
    return jnp.concatenate([jnp.where(lane < HALF_A, q, zero), jnp.where(lane < HALF_A, zero, q)], axis=0)


def _lambda(lq1, lk1, lq2, lk2, lam_init):
    return (jnp.exp(jnp.sum(lq1 * lk1, axis=-1, keepdims=True))
            - jnp.exp(jnp.sum(lq2 * lk2, axis=-1, keepdims=True)) + lam_init)


def _attn_finish(num1, den1, num2, den2, lam, g, lam_init):
    o = num1 / den1 - lam * (num2 / den2)
    ms = jnp.mean(o * o, axis=-1, keepdims=True)
    return o * lax.rsqrt(ms + RMS_EPS) * (g * (1.0 - lam_init))


def _chunk_mask(shape, tq):
    row = lax.broadcasted_iota(jnp.int32, shape, 0)
    col = lax.broadcasted_iota(jnp.int32, shape, 1)
    qc = jnp.where(row >= tq, row - tq, row) // CHUNK_MASK
    return col // CHUNK_MASK <= qc


def _attn_prompt_kernel(q_ref, k_ref, v_ref, lq1, lk1, lq2, lk2, g_ref, o_ref, acc_ref, *, tq, tk, lam_init):
    i = pl.program_id(2)
    qs = _stack_halves(q_ref[...])
    nfull = (i * tq) // tk
    nrest = (i * tq - nfull * tk) // tq

    half = tq // 2

    def tile(start, width, diagonal=False, lhs=qs):
        rows = pl.ds(pl.multiple_of(start, half), width)
        p = jnp.exp2(_dg(lhs, k_ref[rows, :], NT))
        if diagonal:
            p = jnp.where(_chunk_mask(p.shape, lhs.shape[0] // 2), p, 0.0)
        ve = jnp.concatenate([v_ref[rows, :], jnp.ones((width, HEAD_A), BF16)], axis=1)
        return _dg(p.astype(BF16), ve)

    acc_ref[...] = tile(i * tq, half, True)
    late = tile(i * tq + half, half, True, jnp.concatenate([qs[half:tq], qs[tq + half:]], axis=0))
    acc_ref[half:tq, :] += late[:half]
    acc_ref[tq + half:, :] += late[half:]

    def rest(j, c):
        acc_ref[...] += tile(nfull * tk + j * tq, tq, False)
        return c

    def full(j, c):
        acc_ref[...] += tile(j * tk, tk, False)
        return c

    lax.fori_loop(0, nrest, rest, 0)
    lax.fori_loop(0, nfull, full, 0)
    acc = acc_ref[...]
    lam = _lambda(lq1[...], lk1[...], lq2[...], lk2[...], lam_init)
    o = _attn_finish(acc[:tq, :HEAD_A], acc[:tq, HEAD_A:], acc[tq:, :HEAD_A], acc[tq:, HEAD_A:],
                     lam, g_ref[...], lam_init)
    o_ref[...] = o.astype(o_ref.dtype)


def _attn_prompt_safe_kernel(q_ref, k_ref, v_ref, lq1, lk1, lq2, lk2, g_ref, o_ref, *, tq, lam_init):
    i = pl.program_id(2)
    qs = _stack_halves(q_ref[...])

    def tile(j, carry, masked):
        m, l, acc = carry
        start = pl.multiple_of(j * tq, tq)
        k = k_ref[pl.ds(start, tq), :]
        v = v_ref[pl.ds(start, tq), :]
        s = _dg(qs, k, NT)
        if masked:
            s = jnp.where(_chunk_mask(s.shape, tq), s, NEG_BIG)
        m_new = jnp.maximum(m, jnp.max(s, axis=-1, keepdims=True))
        p = jnp.exp2(s - m_new)
        alpha = jnp.exp2(m - m_new)
        l = alpha * l + jnp.sum(p, axis=-1, keepdims=True)
        acc = alpha * acc + _dg(p.astype(BF16), v)
        return m_new, l, acc

    init = (jnp.full((2 * tq, 1), NEG_BIG, F32), jnp.zeros((2 * tq, 1), F32),
            jnp.zeros((2 * tq, HEAD_A), F32))
    carry = lax.fori_loop(0, i, lambda j, c: tile(j, c, False), init)
    _, l, acc = tile(i, carry, True)
    lam = _lambda(lq1[...], lk1[...], lq2[...], lk2[...], lam_init)
    o = _attn_finish(acc[:tq], l[:tq], acc[tq:], l[tq:], lam, g_ref[...], lam_init)
    o_ref[...] = o.astype(o_ref.dtype)


def _attn_prompt(q, k, v, lams, g, lam_init, logit_bound, *, tq, tk):
    b, t, width = q.shape
    nh = width // HEAD_A
    small = pl.BlockSpec((1, HALF_A), lambda bi, h, i: (0, 0))

    def call(body, tq, scratch, name, **kw):
        return pl.pallas_call(
            functools.partial(body, tq=tq, lam_init=lam_init, **kw),
            grid=(b, nh, t // tq),
            in_specs=[
                pl.BlockSpec((None, tq, HEAD_A), lambda bi, h, i: (bi, i, h)),
                pl.BlockSpec((None, t, HEAD_A), lambda bi, h, i: (bi, 0, h)),
                pl.BlockSpec((None, t, HEAD_A), lambda bi, h, i: (bi, 0, h)),
                small, small, small, small,
                pl.BlockSpec((1, HEAD_A), lambda bi, h, i: (0, 0)),
            ],
            out_specs=pl.BlockSpec((None, tq, HEAD_A), lambda bi, h, i: (bi, i, h)),
            out_shape=jax.ShapeDtypeStruct((b, t, width), BF16),
            scratch_shapes=scratch,
            compiler_params=_params("parallel", "parallel", "arbitrary"),
            name=name,
        )

    fast = call(_attn_prompt_kernel, tq, [pltpu.VMEM((2 * tq, 2 * HEAD_A), F32)], "attn_prompt", tk=tk)
    safe = call(_attn_prompt_safe_kernel, _pick(t, SAFE_TQ), [], "attn_prompt_safe")
    return lax.cond(logit_bound <= MAX_UNSHIFTED_LOGIT2, fast, safe, q, k, v, *lams, g)


def _attn_sample_kernel(q_ref, kp_ref, vp_ref, kn_ref, vn_ref, lq1, lk1, lq2, lk2, g_ref, o_ref,
                        *, tq, past, nh, lam_init):
    lam = _lambda(lq1[...], lk1[...], lq2[...], lk2[...], lam_init)
    for h in range(nh):
        cols = slice(h * HEAD_A, (h + 1) * HEAD_A)
        head_rows = pl.ds(h, past, stride=nh)
        qs = _stack_halves(q_ref[:, cols])
        s_p = _dg(qs, kp_ref[head_rows, :].astype(BF16), NT)
        s_n = _dg(qs, kn_ref[:, cols], NT)
        row = lax.broadcasted_iota(jnp.int32, s_n.shape, 0)
        col = lax.broadcasted_iota(jnp.int32, s_n.shape, 1)
        qc = (past + jnp.where(row >= tq, row - tq, row)) // CHUNK_MASK
        s_n = jnp.where((past + col) // CHUNK_MASK <= qc, s_n, NEG_BIG)
        m = jnp.maximum(jnp.max(s_p, axis=-1, keepdims=True), jnp.max(s_n, axis=-1, keepdims=True))
        p_p = jnp.exp2(s_p - m)
        p_n = jnp.exp2(s_n - m)
        l = jnp.sum(p_p, axis=-1, keepdims=True) + jnp.sum(p_n, axis=-1, keepdims=True)
        acc = (_dg(p_p.astype(BF16), vp_ref[head_rows, :].astype(BF16))
               + _dg(p_n.astype(BF16), vn_ref[:, cols]))
        o = _attn_finish(acc[:tq], l[:tq], acc[tq:], l[tq:], lam, g_ref[...], lam_init)
        o_ref[:, cols] = o.astype(o_ref.dtype)


def _attn_sample(q, k_past, v_past, layer, k_new, v_new, lams, g, lam_init):
    b, t, width = q.shape
    depth, _, past, nh, _ = k_past.shape
    k_past = k_past.reshape(depth, b, past * nh, HEAD_A)
    v_past = v_past.reshape(depth, b, past * nh, HEAD_A)
    small = pl.BlockSpec((1, HALF_A), lambda bi: (0, 0))
    new = pl.BlockSpec((None, t, width), lambda bi: (bi, 0, 0))
    old = pl.BlockSpec((None, None, past * nh, HEAD_A), lambda bi: (layer, bi, 0, 0))
    return pl.pallas_call(
        functools.partial(_attn_sample_kernel, tq=t, past=past, nh=nh, lam_init=lam_init),
        grid=(b,),
        in_specs=[new, old, old, new, new, small, small, small, small,
                  pl.BlockSpec((1, HEAD_A), lambda bi: (0, 0))],
        out_specs=new,
        out_shape=jax.ShapeDtypeStruct((b, t, width), BF16),
        compiler_params=_params("parallel"),
        name="attn_sample",
    )(q, k_past, v_past, k_new, v_new, *lams, g)


def _stack_heads(x):
    lane = lax.broadcasted_iota(jnp.int32, x.shape, 1)
    return jnp.concatenate([jnp.where(lane < HEAD_B, x, 0.0), jnp.where(lane < HEAD_B, 0.0, x)], axis=0)


def _rwkv_kernel(pr_ref, pk_ref, pv_ref, pl_ref, sr_ref, sk_ref, sv_ref, sl_ref,
                 mr_ref, mk_ref, mv_ref, ml_ref,
                 w0_ref, a0_ref, kk_ref, ka_ref, rk_ref, lg_ref, lb_ref,
                 w2_ref, a2_ref, g2_ref, s0_ref,
                 out_ref, sout_ref,
                 s_sc, cr_sc, ck_sc, cv_sc, cl_sc, *, tb, ng):
    t = pl.program_id(2)
    width = ng * LANES

    @pl.when(t == 0)
    def _():
        s_sc[...] = s0_ref[...]
        cr_sc[...] = sr_ref[...]
        ck_sc[...] = sk_ref[...]
        cv_sc[...] = sv_ref[...]
        cl_sc[...] = sl_ref[...]

    def shifted(p_ref, carry_ref, mu_ref):
        p = p_ref[...]
        prev = pltpu.roll(p, 1, axis=0)
        row = lax.broadcasted_iota(jnp.int32, p.shape, 0)
        prev = jnp.where(row == 0, carry_ref[...], prev)
        carry_ref[...] = p[tb - 1:tb, :]
        return p + (prev - p) * mu_ref[...]

    xr = shifted(pr_ref, cr_sc, mr_ref)
    xk = shifted(pk_ref, ck_sc, mk_ref)
    xv = shifted(pv_ref, cv_sc, mv_ref)
    xl = shifted(pl_ref, cl_sc, ml_ref)

    slab = min(width, 2 * LANES)

    def head_sums(x, value):
        mat = _group_matrix(slab, HEAD_B, value)
        parts = [_group_sums(x[:, o:o + slab], mat) for o in range(0, width, slab)]
        return parts[0] if len(parts) == 1 else jnp.concatenate(parts, axis=1)

    z = w0_ref[...] + _dg(jnp.tanh(xl).astype(BF16), w2_ref[...])
    logw = -math.exp(-0.5) * _sigmoid(z)
    ag = _sigmoid(a0_ref[...] + _dg(xl.astype(BF16), a2_ref[...]))
    gate = _dg(_sigmoid(xl).astype(BF16), g2_ref[...])
    kk = xk * kk_ref[...]
    kk = kk * lax.rsqrt(jnp.maximum(head_sums(kk * kk, 1.0), 1e-24))
    km = xk * (1.0 + (ag - 1.0) * ka_ref[...])
    bonus = head_sums(xr * km * rk_ref[...], 1.0) * xv

    c = CHUNK_B
    c2 = 2 * c
    ri = lax.broadcasted_iota(jnp.int32, (c, c), 0)
    ci = lax.broadcasted_iota(jnp.int32, (c, c), 1)
    tri = jnp.where(ci <= ri, 1.0, 0.0).astype(BF16)
    r2 = lax.broadcasted_iota(jnp.int32, (c2, c2), 0)
    c2i = lax.broadcasted_iota(jnp.int32, (c2, c2), 1)
    rm = r2 % c
    cm = c2i % c
    strict = cm < rm
    incl = cm <= rm
    eye = jnp.where(r2 == c2i, 1.0, 0.0)

    def off_blocks(s):
        rs = rm // s
        return (rs - cm // s) * 2 + rs % 2 == 3

    nchunk = tb // c
    zero = jnp.zeros((c2, c2), BF16)
    pre = {}

    def prepare(keys):
        for g, n in keys:
            rows = slice(n * c, (n + 1) * c)
            cols = slice(g * LANES, (g + 1) * LANES)
            lw = logw[rows, cols]
            h1 = lw.astype(BF16)
            r1 = lw - h1.astype(F32)
            h2 = r1.astype(BF16)
            h3 = (r1 - h2.astype(F32)).astype(BF16)
            cum = _dg(tri, h1) + (_dg(tri, h2) + _dg(tri, h3))
            w_in = jnp.exp(cum)
            w_ex = jnp.exp(cum - lw)
            w_inv = jnp.exp(-cum)
            kc = kk[rows, cols]
            ar = jnp.concatenate([_stack_heads(-kc * w_ex),
                                  _stack_heads(xr[rows, cols] * w_in)], axis=0).astype(BF16)
            bk = jnp.concatenate([_stack_heads(kc * ag[rows, cols] * w_inv),
                                  _stack_heads(km[rows, cols] * w_inv)], axis=0).astype(BF16)
            v_s = _stack_heads(xv[rows, cols])
            pre[g, n] = dict(ar=ar, bk=bk, v_h=v_s.astype(BF16), vt_h=v_s.T.astype(BF16),
                             w_last=w_in[c - 1:c, :])

        for key in keys:
            d = pre[key]
            gm = _dg(d["ar"], d["bk"], NT)
            d["n_ab"] = jnp.where(strict, gm[:c2, :c2], 0.0).astype(BF16)
            d["n_rb"] = jnp.where(incl, gm[c2:, :c2], 0.0).astype(BF16)
            n_k = jnp.concatenate([jnp.where(strict, gm[:c2, c2:], 0.0),
                                   jnp.where(incl, gm[c2:, c2:], 0.0)], axis=0).astype(BF16)
            kv = _dg(n_k, d["v_h"])
            d["akv"] = kv[:c2].astype(BF16)
            d["yv"] = kv[c2:]
            d["vk"] = _dg(d["vt_h"], d["bk"][c2:])

        for key in keys:
            d = pre[key]
            d["tinv"] = eye + jnp.where(off_blocks(1), d["n_ab"], zero).astype(F32)
        s = 2
        while s < c:
            sel = off_blocks(s)
            for key in keys:
                d = pre[key]
                d["tinv_h"] = d["tinv"].astype(BF16)
                d["tmp"] = _dg(jnp.where(sel, d["n_ab"], zero), d["tinv_h"]).astype(BF16)
            for key in keys:
                d = pre[key]
                d["tinv"] = d["tinv"] + _dg(d["tinv_h"], d["tmp"])
            s *= 2

        for key in keys:
            d = pre[key]
            tx = _dg(d["tinv"].astype(BF16), jnp.concatenate([d["ar"][:c2], d["akv"]], axis=1))
            d["tar"] = jnp.concatenate([tx[:, :c2].astype(BF16), d["ar"][c2:]], axis=0)
            d["u0"] = tx[:, c2:]
            d["u0t"] = tx[:, c2:].T

    state = [s_sc[g] for g in range(ng)]
    ys = {}

    def advance(n):
        for g in range(ng):
            d = pre[g, n]
            s_h = state[g].astype(BF16)
            d["ut"] = _dg(s_h, d["tar"][:c2], NT) + d["u0t"]
            d["us"] = _dg(d["tar"], s_h, NT)
        for g in range(ng):
            d = pre[g, n]
            u1 = d["us"][:c2] + d["u0"]
            y_s = d["us"][c2:] + _dg(d["n_rb"], u1.astype(BF16)) + d["yv"]
            ys[g, n] = y_s[:c] + y_s[c:]
            state[g] = (state[g] + _dg(d["ut"].astype(BF16), d["bk"][:c2]) + d["vk"]) * d["w_last"]

    prepare([(g, n) for n in range(nchunk) for g in range(ng)])
    for n in range(nchunk):
        advance(n)
    for g in range(ng):
        s_sc[g] = state[g]

    @pl.when(t == pl.num_programs(2) - 1)
    def _():
        for g in range(ng):
            sout_ref[g] = state[g]

    y = jnp.concatenate(
        [jnp.concatenate([ys[g, n] for n in range(nchunk)], axis=0) for g in range(ng)], axis=1)
    mean = head_sums(y, 1.0 / HEAD_B)
    dlt = y - mean
    var = head_sums(dlt * dlt, 1.0 / HEAD_B)
    yn = dlt * lax.rsqrt(var + GN_EPS) * lg_ref[...] + lb_ref[...]
    out_ref[...] = ((yn + bonus) * gate).astype(out_ref.dtype)


def _rwkv(p, shift0, s0_bd, mu, vecs, loras, *, tb, ng):
    b, t, _ = p.shape
    width = vecs[0].shape[-1]
    wblk = ng * LANES
    nblk = width // wblk
    lcol = 3 * width // (2 * LANES)

    def col(off):
        return lambda bi, g, ti: (bi, ti, off + g)

    def row(off):
        return lambda bi, g, ti: (bi, 0, off + g)

    def par(off):
        return lambda bi, g, ti: (0, off + g)

    in_specs = (
        [pl.BlockSpec((None, tb, wblk), col(k * nblk)) for k in range(3)]
        + [pl.BlockSpec((None, tb, 2 * LANES), lambda bi, g, ti: (bi, ti, lcol))]
        + [pl.BlockSpec((None, 1, wblk), row(k * nblk)) for k in range(3)]
        + [pl.BlockSpec((None, 1, 2 * LANES), lambda bi, g, ti: (bi, 0, lcol))]
        + [pl.BlockSpec((1, wblk), par(k * nblk)) for k in range(3)]
        + [pl.BlockSpec((1, 2 * LANES), lambda bi, g, ti: (0, lcol))]
        + [pl.BlockSpec((1, wblk), par(0)) for _ in vecs]
        + [pl.BlockSpec((2 * LANES, wblk), par(0)) for _ in loras]
        + [pl.BlockSpec((None, ng, LANES, LANES), lambda bi, g, ti: (bi, g, 0, 0))]
    )
    return pl.pallas_call(
        functools.partial(_rwkv_kernel, tb=tb, ng=ng),
        grid=(b, nblk, t // tb),
        in_specs=in_specs,
        out_specs=[pl.BlockSpec((None, tb, wblk), col(0)),
                   pl.BlockSpec((None, ng, LANES, LANES), lambda bi, g, ti: (bi, g, 0, 0))],
        out_shape=[jax.ShapeDtypeStruct((b, t, width), BF16),
                   jax.ShapeDtypeStruct((b, width // LANES, LANES, LANES), F32)],
        scratch_shapes=[pltpu.VMEM((ng, LANES, LANES), F32), pltpu.VMEM((1, wblk), F32),
                        pltpu.VMEM((1, wblk), F32), pltpu.VMEM((1, wblk), F32),
                        pltpu.VMEM((1, 2 * LANES), F32)],
        compiler_params=_params("parallel", "parallel", "arbitrary"),
        name="rwkv",
    )(p, p, p, p, shift0, shift0, shift0, shift0, mu, mu, mu, mu, *vecs, *loras, s0_bd)


def _merge_ffn_kernel(x_ref, a_ref, b_ref, wo_ref, g_ref, wg_ref, wu_ref, wd_ref, o_ref, hn_ref):
    @pl.when(pl.program_id(1) == 0)
    def _():
        half = a_ref.shape[1]
        h = x_ref[...] + (_dg(a_ref[...], wo_ref[:half, :]) + _dg(b_ref[...], wo_ref[half:, :]))
        ms = jnp.mean(h * h, axis=-1, keepdims=True)
        hn_ref[...] = (h * lax.rsqrt(ms + RMS_EPS) * g_ref[...]).astype(BF16)
        o_ref[...] = h

    hn = hn_ref[...]
    gt = _dg(hn, wg_ref[...])
    up = _dg(hn, wu_ref[...])
    act = gt * _sigmoid(gt) * up
    o_ref[...] += _dg(act.astype(BF16), wd_ref[...])


def _merge_ffn(x2d, a, b, wo, g, wg, wu, wd, *, tm, tf):
    n, d = x2d.shape
    half = a.shape[1]
    dff = wg.shape[1]
    return pl.pallas_call(
        _merge_ffn_kernel,
        grid=(n // tm, dff // tf),
        in_specs=[pl.BlockSpec((tm, d), lambda i, j: (i, 0)),
                  pl.BlockSpec((tm, half), lambda i, j: (i, 0)),
                  pl.BlockSpec((tm, half), lambda i, j: (i, 0)),
                  pl.BlockSpec((d, d), lambda i, j: (0, 0), pipeline_mode=pl.Buffered(1)),
                  pl.BlockSpec((1, d), lambda i, j: (0, 0)),
                  pl.BlockSpec((d, tf), lambda i, j: (0, j)),
                  pl.BlockSpec((d, tf), lambda i, j: (0, j)),
                  pl.BlockSpec((tf, d), lambda i, j: (j, 0))],
        out_specs=pl.BlockSpec((tm, d), lambda i, j: (i, 0)),
        out_shape=jax.ShapeDtypeStruct((n, d), F32),
        scratch_shapes=[pltpu.VMEM((tm, d), BF16)],
        compiler_params=_params("parallel", "arbitrary"),
        name="merge_ffn",
    )(x2d, a, b, wo, g.reshape(1, d), wg, wu, wd)


def _pick(n, pref):
    return pref if n % pref == 0 else n


def _block_diag_state(s):
    b, h = s.shape[:2]
    s = s.reshape(b, h // 2, 2, HEAD_B, HEAD_B)
    z = jnp.zeros_like(s[:, :, 0])
    top = jnp.concatenate([s[:, :, 0], z], axis=-1)
    bot = jnp.concatenate([z, s[:, :, 1]], axis=-1)
    return jnp.concatenate([top, bot], axis=-2)


def _unblock_state(s_bd):
    b, g = s_bd.shape[:2]
    s = jnp.stack([s_bd[:, :, :HEAD_B, :HEAD_B], s_bd[:, :, HEAD_B:, HEAD_B:]], axis=2)
    return s.reshape(b, 2 * g, HEAD_B, HEAD_B)


def _layer(x, pos, past, s0, shift0, lam_init, wts):
    b, t, d = x.shape
    n = b * t
    wa = wts["width_a"]
    wb = wts["width_b"]
    x2d = x.reshape(n, d)
    tm = _pick(n, 512)
    tm_in = _pick(n, 1024)
    rope = _rope_tables(pos, max(t, tm_in))
    lams = wts["lams"]

    q, k32, k16, v32, v16, p_rwkv = _inproj(
        x2d, wts["norm1_g"], wts["w_in"], wts["q_norm_g"], wts["k_norm_g"], rope,
        width_a=wa, q_scale=HALF_A ** -0.5 * LOG2E, tm=tm_in, tn=INPROJ_TN)

    q = q.reshape(b, t, wa)
    k16 = k16.reshape(b, t, wa)
    v16 = v16.reshape(b, t, wa)
    if past is None:
        bound = (8.0 * 1.01 * LOG2E) * jnp.max(jnp.abs(wts["q_norm_g"])) * jnp.max(jnp.abs(wts["k_norm_g"]))
        attn = _attn_prompt(q, k16, v16, lams, wts["subln_g"], lam_init, bound,
                            tq=_pick(t, 1024), tk=_pick(t, 2048))
        tb, ng = _pick(t, 256), 4
    else:
        attn = _attn_sample(q, *past, k16, v16, lams, wts["subln_g"], lam_init)
        tb, ng = _pick(t, CHUNK_B), wb // LANES

    p_rwkv = p_rwkv.reshape(b, t, -1)
    rw, s_bd = _rwkv(p_rwkv, shift0, s0, wts["mu"], wts["vecs"], wts["loras"], tb=tb, ng=ng)

    y = _merge_ffn(x2d, attn.reshape(n, wa), rw.reshape(n, wb), wts["w_out"], wts["norm2_g"],
                   wts["w_gate"], wts["w_up"], wts["w_down"], tm=tm, tf=512)

    nh_a = wa // HEAD_A
    rcols = wts["rwkv_cols"]
    return (y.reshape(b, t, d), k32.reshape(b, t, nh_a, HEAD_A), v32.reshape(b, t, nh_a, HEAD_A),
            _unblock_state(s_bd), p_rwkv[:, t - 1:, :rcols])


def kernel(x_prompt, x_sample, cache_attn_k, cache_attn_v, state_rwkv, state_rwkv_shift, norm1_g, w_in, q_norm_g, k_norm_g, lambda_q1, lambda_k1, lambda_q2, lambda_k2, subln_g, mu_rwkv, w0, w2, a0, a2, g2, k_k, k_a, r_k, lnx_g, lnx_b, w_out, norm2_g, w_gate, w_up, w_down):
    depth = w_in.shape[0]
    bp, tp, d = x_prompt.shape
    bs, ts, _ = x_sample.shape
    past_len = cache_attn_k.shape[2]
    nh_a = cache_attn_k.shape[3]
    wa = nh_a * HEAD_A
    wb = d - wa
    rcols = 3 * wb + 3 * LORA
    pad = -rcols % INPROJ_TN
    pos_p = jnp.arange(tp, dtype=jnp.int32)
    pos_s = past_len + jnp.arange(ts, dtype=jnp.int32)

    xp, xs = x_prompt, x_sample
    outs = [[] for _ in range(8)]
    for l in range(depth):
        lam_init = 0.8 - 0.6 * math.exp(-0.3 * l)

        def lora_pad(w, slot):
            return jnp.pad(w, ((slot * LORA, 2 * LANES - (slot + 1) * LORA), (0, 0)))

        row = lambda v: v.reshape(1, -1)
        wts = dict(
            width_a=wa, width_b=wb, rwkv_cols=rcols,
            norm1_g=norm1_g[l], norm2_g=norm2_g[l],
            w_in=jnp.pad(w_in[l], ((0, 0), (0, pad))).astype(BF16),
            q_norm_g=row(q_norm_g[l]), k_norm_g=row(k_norm_g[l]),
            lams=[row(lambda_q1[l]), row(lambda_k1[l]), row(lambda_q2[l]), row(lambda_k2[l])],
            subln_g=row(subln_g[l]),
            mu=jnp.pad(row(mu_rwkv[l]), ((0, 0), (0, pad))),
            vecs=[row(w0[l]), row(a0[l]), row(k_k[l]), row(k_a[l]), row(r_k[l]), row(lnx_g[l]), row(lnx_b[l])],
            loras=[lora_pad(w2[l], 0).astype(BF16), lora_pad(a2[l], 1).astype(BF16),
                   lora_pad(g2[l], 2).astype(BF16)],
            w_out=w_out[l].astype(BF16), w_gate=w_gate[l].astype(BF16),
            w_up=w_up[l].astype(BF16), w_down=w_down[l].astype(BF16),
        )
        s0_p = jnp.zeros((bp, wb // LANES, LANES, LANES), F32)
        shift0_p = jnp.zeros((bp, 1, rcols + pad), F32)
        xp, kp, vp, sp, shp = _layer(xp, pos_p, None, s0_p, shift0_p, lam_init, wts)
        past = (cache_attn_k, cache_attn_v, l)
        shift0_s = jnp.pad(state_rwkv_shift[l], ((0, 0), (0, 0), (0, pad)))
        xs, ks, vs, ss, shs = _layer(xs, pos_s, past, _block_diag_state(state_rwkv[l]), shift0_s,
                                     lam_init, wts)
        for lst, val in zip(outs, (kp, vp, sp, shp, ks, vs, ss, shs)):
            lst.append(val)
    return (xp, xs) + tuple(jnp.stack(o) for o in outs)
```

```python
import functools
import math

import jax
import jax.numpy as jnp
from jax import lax
from jax.experimental import pallas as pl
from jax.experimental.pallas import tpu as pltpu

F32 = jnp.float32
BF16 = jnp.bfloat16

LANES = 128
HEAD_A = 128
HALF_A = 64
ROT_HALF = 8
ROPE_THETA = 500000.0
HEAD_B = 64
CHUNK_MASK = 64
CHUNK_B = 64
LORA = 64
RMS_EPS = 1e-6
GN_EPS = 64e-5
NEG_BIG = -1e30
LOG2E = 1.4426950408889634
MAX_UNSHIFTED_LOGIT2 = 64.0
VMEM_LIMIT = 56 * 1024 * 1024
INPROJ_TN = 512
ROPE_ROW_SLABS = 4
SAFE_TQ = 256

NN = ((1,), (0,))
NT = ((1,), (1,))


def _dg(a, b, dims=NN):
    return lax.dot_general(a, b, (dims, ((), ())), preferred_element_type=F32)


def _group_sums(a, group_matrix):
    return _dg(a.astype(BF16), group_matrix)


def _group_matrix(n, group, value):
    r = lax.broadcasted_iota(jnp.int32, (n, n), 0) // group
    c = lax.broadcasted_iota(jnp.int32, (n, n), 1) // group
    return jnp.where(r == c, value, 0.0).astype(BF16)


def _sigmoid(x):
    return 1.0 / (1.0 + jnp.exp(-x))


def _params(*sem):
    return pltpu.CompilerParams(dimension_semantics=sem, vmem_limit_bytes=VMEM_LIMIT)


def _inproj_kernel(x_ref, g_ref, w_ref, qg_ref, kg_ref, cos_ref, sin_ref,
                   q_ref, k32_ref, k16_ref, v32_ref, v16_ref, p_ref, xn_ref, *, nsec, q_scale):
    j = pl.program_id(1)

    @pl.when(j == 0)
    def _():
        x = x_ref[...]
        ms = jnp.mean(x * x, axis=-1, keepdims=True)
        xn_ref[...] = (x * lax.rsqrt(ms + RMS_EPS) * g_ref[...]).astype(BF16)

    def project(rows=slice(None)):
        return _dg(xn_ref[rows, :], w_ref[...])

    @pl.when(j < 2 * nsec)
    def _():
        is_q = j < nsec
        seg = _group_matrix(LANES, HALF_A, 1.0 / HALF_A)
        hg = jnp.where(is_q, qg_ref[...] * q_scale, kg_ref[...])
        lane = lax.broadcasted_iota(jnp.int32, (1, LANES), 1) % HALF_A
        step = xn_ref.shape[0] // ROPE_ROW_SLABS
        parts = []
        for r in range(ROPE_ROW_SLABS):
            rows = slice(r * step, (r + 1) * step)
            acc = project(rows)
            cos, sin = cos_ref[rows, :], sin_ref[rows, :]
            sa = jnp.where(lane < ROT_HALF, sin, 0.0)
            sb = jnp.where(lane < ROT_HALF, 0.0, sin)
            groups = []
            for g in range(acc.shape[1] // LANES):
                xg = acc[:, g * LANES:(g + 1) * LANES]
                ms = _group_sums(xg * xg, seg)
                xg = xg * lax.rsqrt(ms + RMS_EPS) * hg
                up = pltpu.roll(xg, LANES - ROT_HALF, axis=1)
                dn = pltpu.roll(xg, ROT_HALF, axis=1)
                groups.append(xg * cos + up * sa + dn * sb)
            parts.append(jnp.concatenate(groups, axis=1))
        rot = jnp.concatenate(parts, axis=0)

        @pl.when(is_q)
        def _():
            q_ref[...] = rot.astype(BF16)

        @pl.when(jnp.logical_not(is_q))
        def _():
            k32_ref[...] = rot
            k16_ref[...] = rot.astype(BF16)

    @pl.when(jnp.logical_and(j >= 2 * nsec, j < 3 * nsec))
    def _():
        acc = project()
        v32_ref[...] = acc
        v16_ref[...] = acc.astype(BF16)

    @pl.when(j >= 3 * nsec)
    def _():
        p_ref[...] = project()


def _inproj(x2d, g, w, qg, kg, rope, *, width_a, q_scale, tm, tn):
    n, d = x2d.shape
    ncol = w.shape[1]
    nsec = width_a // tn
    nblk = rope[0].shape[0] // tm
    sec = lambda k: (lambda i, j: (i, jnp.clip(j - k * nsec, 0, nsec - 1)))
    vec = pl.BlockSpec((1, LANES), lambda i, j: (0, 0))
    tab = pl.BlockSpec((tm, LANES), lambda i, j: (i % nblk, 0))
    blk = lambda imap: pl.BlockSpec((tm, tn), imap)
    return pl.pallas_call(
        functools.partial(_inproj_kernel, nsec=nsec, q_scale=q_scale),
        grid=(n // tm, ncol // tn),
        in_specs=[pl.BlockSpec((tm, d), lambda i, j: (i, 0)),
                  pl.BlockSpec((1, d), lambda i, j: (0, 0)),
                  pl.BlockSpec((d, tn), lambda i, j: (0, j)),
                  vec, vec, tab, tab],
        out_specs=[blk(sec(0)), blk(sec(1)), blk(sec(1)), blk(sec(2)), blk(sec(2)),
                   blk(lambda i, j: (i, jnp.maximum(j - 3 * nsec, 0)))],
        out_shape=[jax.ShapeDtypeStruct((n, width_a), BF16),
                   jax.ShapeDtypeStruct((n, width_a), F32), jax.ShapeDtypeStruct((n, width_a), BF16),
                   jax.ShapeDtypeStruct((n, width_a), F32), jax.ShapeDtypeStruct((n, width_a), BF16),
                   jax.ShapeDtypeStruct((n, ncol - 3 * width_a), F32)],
        scratch_shapes=[pltpu.VMEM((tm, d), BF16)],
        compiler_params=_params("parallel", "arbitrary"),
        name="inproj",
    )(x2d, g.reshape(1, d), w, qg, kg, *rope)


def _rope_tables(pos, rows):
    inv = jnp.float32(ROPE_THETA) ** (-jnp.arange(0, 2 * ROT_HALF, 2, dtype=F32) / (2 * ROT_HALF))
    lane = jnp.arange(HEAD_A) % HALF_A
    inv_lane = jnp.where(lane < 2 * ROT_HALF, inv[lane % ROT_HALF], 0.0)
    sign = jnp.where(lane < ROT_HALF, -1.0, 1.0)
    ang = pos.astype(F32)[:, None] * inv_lane[None, :]
    tabs = [jnp.cos(ang), jnp.sin(ang) * sign]
    t = pos.shape[0]
    if rows > t:
        tabs = [jnp.tile(h, (rows // t, 1)) for h in tabs]
    return tabs


def _stack_halves(q):
    lane = lax.broadcasted_iota(jnp.int32, q.shape, 1)
    zero = jnp.zeros_like(q)
    return jnp.concatenate([jnp.where(lane < HALF_A, q, zero), jnp.where(lane < HALF_A, zero, q)], axis=0)


def _lambda(lq1, lk1, lq2, lk2, lam_init):
    return (jnp.exp(jnp.sum(lq1 * lk1, axis=-1, keepdims=True))
            - jnp.exp(jnp.sum(lq2 * lk2, axis=-1, keepdims=True)) + lam_init)


def _attn_finish(num1, den1, num2, den2, lam, g, lam_init):
    o = num1 / den1 - lam * (num2 / den2)
    ms = jnp.mean(o * o, axis=-1, keepdims=True)
    return o * lax.rsqrt(ms + RMS_EPS) * (g * (1.0 - lam_init))


def _chunk_mask(shape, tq):
    row = lax.broadcasted_iota(jnp.int32, shape, 0)
    col = lax.broadcasted_iota(jnp.int32, shape, 1)
    qc = jnp.where(row >= tq, row - tq, row) // CHUNK_MASK
    return col // CHUNK_MASK <= qc


def _attn_prompt_kernel(q_ref, k_ref, v_ref, lq1, lk1, lq2, lk2, g_ref, o_ref, acc_ref, *, tq, tk, lam_init):
    i = pl.program_id(2)
    qs = _stack_halves(q_ref[...])
    nfull = (i * tq) // tk
    nrest = (i * tq - nfull * tk) // tq

    half = tq // 2

    def tile(start, width, diagonal=False, lhs=qs):
        rows = pl.ds(pl.multiple_of(start, half), width)
        p = jnp.exp2(_dg(lhs, k_ref[rows, :], NT))
        if diagonal:
            p = jnp.where(_chunk_mask(p.shape, lhs.shape[0] // 2), p, 0.0)
        ve = jnp.concatenate([v_ref[rows, :], jnp.ones((width, HEAD_A), BF16)], axis=1)
        return _dg(p.astype(BF16), ve)

    acc_ref[...] = tile(i * tq, half, True)
    late = tile(i * tq + half, half, True, jnp.concatenate([qs[half:tq], qs[tq + half:]], axis=0))
    acc_ref[half:tq, :] += late[:half]
    acc_ref[tq + half:, :] += late[half:]

    def rest(j, c):
        acc_ref[...] += tile(nfull * tk + j * tq, tq, False)
        return c

    def full(j, c):
        acc_ref[...] += tile(j * tk, tk, False)
        return c

    lax.fori_loop(0, nrest, rest, 0)
    lax.fori_loop(0, nfull, full, 0)
    acc = acc_ref[...]
    lam = _lambda(lq1[...], lk1[...], lq2[...], lk2[...], lam_init)
    o = _attn_finish(acc[:tq, :HEAD_A], acc[:tq, HEAD_A:], acc[tq:, :HEAD_A], acc[tq:, HEAD_A:],
                     lam, g_ref[...], lam_init)
    o_ref[...] = o.astype(o_ref.dtype)


def _attn_prompt_safe_kernel(q_ref, k_ref, v_ref, lq1, lk1, lq2, lk2, g_ref, o_ref, *, tq, lam_init):
    i = pl.program_id(2)
    qs = _stack_halves(q_ref[...])

    def tile(j, carry, masked):
        m, l, acc = carry
        start = pl.multiple_of(j * tq, tq)
        k = k_ref[pl.ds(start, tq), :]
        v = v_ref[pl.ds(start, tq), :]
        s = _dg(qs, k, NT)
        if masked:
            s = jnp.where(_chunk_mask(s.shape, tq), s, NEG_BIG)
        m_new = jnp.maximum(m, jnp.max(s, axis=-1, keepdims=True))
        p = jnp.exp2(s - m_new)
        alpha = jnp.exp2(m - m_new)
        l = alpha * l + jnp.sum(p, axis=-1, keepdims=True)
        acc = alpha * acc + _dg(p.astype(BF16), v)
        return m_new, l, acc

    init = (jnp.full((2 * tq, 1), NEG_BIG, F32), jnp.zeros((2 * tq, 1), F32),
            jnp.zeros((2 * tq, HEAD_A), F32))
    carry = lax.fori_loop(0, i, lambda j, c: tile(j, c, False), init)
    _, l, acc = tile(i, carry, True)
    lam = _lambda(lq1[...], lk1[...], lq2[...], lk2[...], lam_init)
    o = _attn_finish(acc[:tq], l[:tq], acc[tq:], l[tq:], lam, g_ref[...], lam_init)
    o_ref[...] = o.astype(o_ref.dtype)


def _attn_prompt(q, k, v, lams, g, lam_init, logit_bound, *, tq, tk):
    b, t, width = q.shape
    nh = width // HEAD_A
    small = pl.BlockSpec((1, HALF_A), lambda bi, h, i: (0, 0))

    def call(body, tq, scratch, name, **kw):
        return pl.pallas_call(
            functools.partial(body, tq=tq, lam_init=lam_init, **kw),
            grid=(b, nh, t // tq),
            in_specs=[
                pl.BlockSpec((None, tq, HEAD_A), lambda bi, h, i: (bi, i, h)),
                pl.BlockSpec((None, t, HEAD_A), lambda bi, h, i: (bi, 0, h)),
                pl.BlockSpec((None, t, HEAD_A), lambda bi, h, i: (bi, 0, h)),
                small, small, small, small,
                pl.BlockSpec((1, HEAD_A), lambda bi, h, i: (0, 0)),
            ],
            out_specs=pl.BlockSpec((None, tq, HEAD_A), lambda bi, h, i: (bi, i, h)),
            out_shape=jax.ShapeDtypeStruct((b, t, width), BF16),
            scratch_shapes=scratch,
            compiler_params=_params("parallel", "parallel", "arbitrary"),
            name=name,
        )

    fast = call(_attn_prompt_kernel, tq, [pltpu.VMEM((2 * tq, 2 * HEAD_A), F32)], "attn_prompt", tk=tk)
    safe = call(_attn_prompt_safe_kernel, _pick(t, SAFE_TQ), [], "attn_prompt_safe")
    return lax.cond(logit_bound <= MAX_UNSHIFTED_LOGIT2, fast, safe, q, k, v, *lams, g)


def _attn_sample_kernel(q_ref, kp_ref, vp_ref, kn_ref, vn_ref, lq1, lk1, lq2, lk2, g_ref, o_ref,
                        *, tq, past, nh, lam_init):
    lam = _lambda(lq1[...], lk1[...], lq2[...], lk2[...], lam_init)
    for h in range(nh):
        cols = slice(h * HEAD_A, (h + 1) * HEAD_A)
        head_rows = pl.ds(h, past, stride=nh)
        qs = _stack_halves(q_ref[:, cols])
        s_p = _dg(qs, kp_ref[head_rows, :].astype(BF16), NT)
        s_n = _dg(qs, kn_ref[:, cols], NT)
        row = lax.broadcasted_iota(jnp.int32, s_n.shape, 0)
        col = lax.broadcasted_iota(jnp.int32, s_n.shape, 1)
        qc = (past + jnp.where(row >= tq, row - tq, row)) // CHUNK_MASK
        s_n = jnp.where((past + col) // CHUNK_MASK <= qc, s_n, NEG_BIG)
        m = jnp.maximum(jnp.max(s_p, axis=-1, keepdims=True), jnp.max(s_n, axis=-1, keepdims=True))
        p_p = jnp.exp2(s_p - m)
        p_n = jnp.exp2(s_n - m)
        l = jnp.sum(p_p, axis=-1, keepdims=True) + jnp.sum(p_n, axis=-1, keepdims=True)
        acc = (_dg(p_p.astype(BF16), vp_ref[head_rows, :].astype(BF16))
               + _dg(p_n.astype(BF16), vn_ref[:, cols]))
        o = _attn_finish(acc[:tq], l[:tq], acc[tq:], l[tq:], lam, g_ref[...], lam_init)
        o_ref[:, cols] = o.astype(o_ref.dtype)


def _attn_sample(q, k_past, v_past, layer, k_new, v_new, lams, g, lam_init):
    b, t, width = q.shape
    depth, _, past, nh, _ = k_past.shape
    k_past = k_past.reshape(depth, b, past * nh, HEAD_A)
    v_past = v_past.reshape(depth, b, past * nh, HEAD_A)
    small = pl.BlockSpec((1, HALF_A), lambda bi: (0, 0))
    new = pl.BlockSpec((None, t, width), lambda bi: (bi, 0, 0))
    old = pl.BlockSpec((None, None, past * nh, HEAD_A), lambda bi: (layer, bi, 0, 0))
    return pl.pallas_call(
        functools.partial(_attn_sample_kernel, tq=t, past=past, nh=nh, lam_init=lam_init),
        grid=(b,),
        in_specs=[new, old, old, new, new, small, small, small, small,
                  pl.BlockSpec((1, HEAD_A), lambda bi: (0, 0))],
        out_specs=new,
        out_shape=jax.ShapeDtypeStruct((b, t, width), BF16),
        compiler_params=_params("parallel"),
        name="attn_sample",
    )(q, k_past, v_past, k_new, v_new, *lams, g)


def _stack_heads(x):
    lane = lax.broadcasted_iota(jnp.int32, x.shape, 1)
    return jnp.concatenate([jnp.where(lane < HEAD_B, x, 0.0), jnp.where(lane < HEAD_B, 0.0, x)], axis=0)


def _rwkv_kernel(pr_ref, pk_ref, pv_ref, pl_ref, sr_ref, sk_ref, sv_ref, sl_ref,
                 mr_ref, mk_ref, mv_ref, ml_ref,
                 w0_ref, a0_ref, kk_ref, ka_ref, rk_ref, lg_ref, lb_ref,
                 w2_ref, a2_ref, g2_ref, s0_ref,
                 out_ref, sout_ref,
                 s_sc, cr_sc, ck_sc, cv_sc, cl_sc, *, tb, ng):
    t = pl.program_id(2)
    width = ng * LANES

    @pl.when(t == 0)
    def _():
        s_sc[...] = s0_ref[...]
        cr_sc[...] = sr_ref[...]
        ck_sc[...] = sk_ref[...]
        cv_sc[...] = sv_ref[...]
        cl_sc[...] = sl_ref[...]

    def shifted(p_ref, carry_ref, mu_ref):
        p = p_ref[...]
        prev = pltpu.roll(p, 1, axis=0)
        row = lax.broadcasted_iota(jnp.int32, p.shape, 0)
        prev = jnp.where(row == 0, carry_ref[...], prev)
        carry_ref[...] = p[tb - 1:tb, :]
        return p + (prev - p) * mu_ref[...]

    xr = shifted(pr_ref, cr_sc, mr_ref)
    xk = shifted(pk_ref, ck_sc, mk_ref)
    xv = shifted(pv_ref, cv_sc, mv_ref)
    xl = shifted(pl_ref, cl_sc, ml_ref)

    slab = min(width, 2 * LANES)

    def head_sums(x, value):
        mat = _group_matrix(slab, HEAD_B, value)
        parts = [_group_sums(x[:, o:o + slab], mat) for o in range(0, width, slab)]
        return parts[0] if len(parts) == 1 else jnp.concatenate(parts, axis=1)

    z = w0_ref[...] + _dg(jnp.tanh(xl).astype(BF16), w2_ref[...])
    logw = -math.exp(-0.5) * _sigmoid(z)
    ag = _sigmoid(a0_ref[...] + _dg(xl.astype(BF16), a2_ref[...]))
    gate = _dg(_sigmoid(xl).astype(BF16), g2_ref[...])
    kk = xk * kk_ref[...]
    kk = kk * lax.rsqrt(jnp.maximum(head_sums(kk * kk, 1.0), 1e-24))
    km = xk * (1.0 + (ag - 1.0) * ka_ref[...])
    bonus = head_sums(xr * km * rk_ref[...], 1.0) * xv

    c = CHUNK_B
    c2 = 2 * c
    ri = lax.broadcasted_iota(jnp.int32, (c, c), 0)
    ci = lax.broadcasted_iota(jnp.int32, (c, c), 1)
    tri = jnp.where(ci <= ri, 1.0, 0.0).astype(BF16)
    r2 = lax.broadcasted_iota(jnp.int32, (c2, c2), 0)
    c2i = lax.broadcasted_iota(jnp.int32, (c2, c2), 1)
    rm = r2 % c
    cm = c2i % c
    strict = cm < rm
    incl = cm <= rm
    eye = jnp.where(r2 == c2i, 1.0, 0.0)

    def off_blocks(s):
        rs = rm // s
        return (rs - cm // s) * 2 + rs % 2 == 3

    nchunk = tb // c
    zero = jnp.zeros((c2, c2), BF16)
    pre = {}

    def prepare(keys):
        for g, n in keys:
            rows = slice(n * c, (n + 1) * c)
            cols = slice(g * LANES, (g + 1) * LANES)
            lw = logw[rows, cols]
            h1 = lw.astype(BF16)
            r1 = lw - h1.astype(F32)
            h2 = r1.astype(BF16)
            h3 = (r1 - h2.astype(F32)).astype(BF16)
            cum = _dg(tri, h1) + (_dg(tri, h2) + _dg(tri, h3))
            w_in = jnp.exp(cum)
            w_ex = jnp.exp(cum - lw)
            w_inv = jnp.exp(-cum)
            kc = kk[rows, cols]
            ar = jnp.concatenate([_stack_heads(-kc * w_ex),
                                  _stack_heads(xr[rows, cols] * w_in)], axis=0).astype(BF16)
            bk = jnp.concatenate([_stack_heads(kc * ag[rows, cols] * w_inv),
                                  _stack_heads(km[rows, cols] * w_inv)], axis=0).astype(BF16)
            v_s = _stack_heads(xv[rows, cols])
            pre[g, n] = dict(ar=ar, bk=bk, v_h=v_s.astype(BF16), vt_h=v_s.T.astype(BF16),
                             w_last=w_in[c - 1:c, :])

        for key in keys:
            d = pre[key]
            gm = _dg(d["ar"], d["bk"], NT)
            d["n_ab"] = jnp.where(strict, gm[:c2, :c2], 0.0).astype(BF16)
            d["n_rb"] = jnp.where(incl, gm[c2:, :c2], 0.0).astype(BF16)
            n_k = jnp.concatenate([jnp.where(strict, gm[:c2, c2:], 0.0),
                                   jnp.where(incl, gm[c2:, c2:], 0.0)], axis=0).astype(BF16)
            kv = _dg(n_k, d["v_h"])
            d["akv"] = kv[:c2].astype(BF16)
            d["yv"] = kv[c2:]
            d["vk"] = _dg(d["vt_h"], d["bk"][c2:])

        for key in keys:
            d = pre[key]
            d["tinv"] = eye + jnp.where(off_blocks(1), d["n_ab"], zero).astype(F32)
        s = 2
        while s < c:
            sel = off_blocks(s)
            for key in keys:
                d = pre[key]
                d["tinv_h"] = d["tinv"].astype(BF16)
                d["tmp"] = _dg(jnp.where(sel, d["n_ab"], zero), d["tinv_h"]).astype(BF16)
            for key in keys:
                d = pre[key]
                d["tinv"] = d["tinv"] + _dg(d["tinv_h"], d["tmp"])
            s *= 2

        for key in keys:
            d = pre[key]
            tx = _dg(d["tinv"].astype(BF16), jnp.concatenate([d["ar"][:c2], d["akv"]], axis=1))
            d["tar"] = jnp.concatenate([tx[:, :c2].astype(BF16), d["ar"][c2:]], axis=0)
            d["u0"] = tx[:, c2:]
            d["u0t"] = tx[:, c2:].T

    state = [s_sc[g] for g in range(ng)]
    ys = {}

    def advance(n):
        for g in range(ng):
            d = pre[g, n]
            s_h = state[g].astype(BF16)
            d["ut"] = _dg(s_h, d["tar"][:c2], NT) + d["u0t"]
            d["us"] = _dg(d["tar"], s_h, NT)
        for g in range(ng):
            d = pre[g, n]
            u1 = d["us"][:c2] + d["u0"]
            y_s = d["us"][c2:] + _dg(d["n_rb"], u1.astype(BF16)) + d["yv"]
            ys[g, n] = y_s[:c] + y_s[c:]
            state[g] = (state[g] + _dg(d["ut"].astype(BF16), d["bk"][:c2]) + d["vk"]) * d["w_last"]

    prepare([(g, n) for n in range(nchunk) for g in range(ng)])
    for n in range(nchunk):
        advance(n)
    for g in range(ng):
        s_sc[g] = state[g]

    @pl.when(t == pl.num_programs(2) - 1)
    def _():
        for g in range(ng):
            sout_ref[g] = state[g]

    y = jnp.concatenate(
        [jnp.concatenate([ys[g, n] for n in range(nchunk)], axis=0) for g in range(ng)], axis=1)
    mean = head_sums(y, 1.0 / HEAD_B)
    dlt = y - mean
    var = head_sums(dlt * dlt, 1.0 / HEAD_B)
    yn = dlt * lax.rsqrt(var + GN_EPS) * lg_ref[...] + lb_ref[...]
    out_ref[...] = ((yn + bonus) * gate).astype(out_ref.dtype)


def _rwkv(p, shift0, s0_bd, mu, vecs, loras, *, tb, ng):
    b, t, _ = p.shape
    width = vecs[0].shape[-1]
    wblk = ng * LANES
    nblk = width // wblk
    lcol = 3 * width // (2 * LANES)

    def col(off):
        return lambda bi, g, ti: (bi, ti, off + g)

    def row(off):
        return lambda bi, g, ti: (bi, 0, off + g)

    def par(off):
        return lambda bi, g, ti: (0, off + g)

    in_specs = (
        [pl.BlockSpec((None, tb, wblk), col(k * nblk)) for k in range(3)]
        + [pl.BlockSpec((None, tb, 2 * LANES), lambda bi, g, ti: (bi, ti, lcol))]
        + [pl.BlockSpec((None, 1, wblk), row(k * nblk)) for k in range(3)]
        + [pl.BlockSpec((None, 1, 2 * LANES), lambda bi, g, ti: (bi, 0, lcol))]
        + [pl.BlockSpec((1, wblk), par(k * nblk)) for k in range(3)]
        + [pl.BlockSpec((1, 2 * LANES), lambda bi, g, ti: (0, lcol))]
        + [pl.BlockSpec((1, wblk), par(0)) for _ in vecs]
        + [pl.BlockSpec((2 * LANES, wblk), par(0)) for _ in loras]
        + [pl.BlockSpec((None, ng, LANES, LANES), lambda bi, g, ti: (bi, g, 0, 0))]
    )
    return pl.pallas_call(
        functools.partial(_rwkv_kernel, tb=tb, ng=ng),
        grid=(b, nblk, t // tb),
        in_specs=in_specs,
        out_specs=[pl.BlockSpec((None, tb, wblk), col(0)),
                   pl.BlockSpec((None, ng, LANES, LANES), lambda bi, g, ti: (bi, g, 0, 0))],
        out_shape=[jax.ShapeDtypeStruct((b, t, width), BF16),
                   jax.ShapeDtypeStruct((b, width // LANES, LANES, LANES), F32)],
        scratch_shapes=[pltpu.VMEM((ng, LANES, LANES), F32), pltpu.VMEM((1, wblk), F32),
                        pltpu.VMEM((1, wblk), F32), pltpu.VMEM((1, wblk), F32),
                        pltpu.VMEM((1, 2 * LANES), F32)],
        compiler_params=_params("parallel", "parallel", "arbitrary"),
        name="rwkv",
    )(p, p, p, p, shift0, shift0, shift0, shift0, mu, mu, mu, mu, *vecs, *loras, s0_bd)


def _merge_ffn_kernel(x_ref, a_ref, b_ref, wo_ref, g_ref, wg_ref, wu_ref, wd_ref, o_ref, hn_ref):
    @pl.when(pl.program_id(1) == 0)
    def _():
        half = a_ref.shape[1]
        h = x_ref[...] + (_dg(a_ref[...], wo_ref[:half, :]) + _dg(b_ref[...], wo_ref[half:, :]))
        ms = jnp.mean(h * h, axis=-1, keepdims=True)
        hn_ref[...] = (h * lax.rsqrt(ms + RMS_EPS) * g_ref[...]).astype(BF16)
        o_ref[...] = h

    hn = hn_ref[...]
    gt = _dg(hn, wg_ref[...])
    up = _dg(hn, wu_ref[...])
    act = gt * _sigmoid(gt) * up
    o_ref[...] += _dg(act.astype(BF16), wd_ref[...])


def _merge_ffn(x2d, a, b, wo, g, wg, wu, wd, *, tm, tf):
    n, d = x2d.shape
    half = a.shape[1]
    dff = wg.shape[1]
    return pl.pallas_call(
        _merge_ffn_kernel,
        grid=(n // tm, dff // tf),
        in_specs=[pl.BlockSpec((tm, d), lambda i, j: (i, 0)),
                  pl.BlockSpec((tm, half), lambda i, j: (i, 0)),
                  pl.BlockSpec((tm, half), lambda i, j: (i, 0)),
                  pl.BlockSpec((d, d), lambda i, j: (0, 0), pipeline_mode=pl.Buffered(1)),
                  pl.BlockSpec((1, d), lambda i, j: (0, 0)),
                  pl.BlockSpec((d, tf), lambda i, j: (0, j)),
                  pl.BlockSpec((d, tf), lambda i, j: (0, j)),
                  pl.BlockSpec((tf, d), lambda i, j: (j, 0))],
        out_specs=pl.BlockSpec((tm, d), lambda i, j: (i, 0)),
        out_shape=jax.ShapeDtypeStruct((n, d), F32),
        scratch_shapes=[pltpu.VMEM((tm, d), BF16)],
        compiler_params=_params("parallel", "arbitrary"),
        name="merge_ffn",
    )(x2d, a, b, wo, g.reshape(1, d), wg, wu, wd)


def _pick(n, pref):
    return pref if n % pref == 0 else n


def _block_diag_state(s):
    b, h = s.shape[:2]
    s = s.reshape(b, h // 2, 2, HEAD_B, HEAD_B)
    z = jnp.zeros_like(s[:, :, 0])
    top = jnp.concatenate([s[:, :, 0], z], axis=-1)
    bot = jnp.concatenate([z, s[:, :, 1]], axis=-1)
    return jnp.concatenate([top, bot], axis=-2)


def _unblock_state(s_bd):
    b, g = s_bd.shape[:2]
    s = jnp.stack([s_bd[:, :, :HEAD_B, :HEAD_B], s_bd[:, :, HEAD_B:, HEAD_B:]], axis=2)
    return s.reshape(b, 2 * g, HEAD_B, HEAD_B)


def _layer(x, pos, past, s0, shift0, lam_init, wts):
    b, t, d = x.shape
    n = b * t
    wa = wts["width_a"]
    wb = wts["width_b"]
    x2d = x.reshape(n, d)
    tm = _pick(n, 512)
    tm_in = _pick(n, 1024)
    rope = _rope_tables(pos, max(t, tm_in))
    lams = wts["lams"]

    q, k32, k16, v32, v16, p_rwkv = _inproj(
        x2d, wts["norm1_g"], wts["w_in"], wts["q_norm_g"], wts["k_norm_g"], rope,
        width_a=wa, q_scale=HALF_A ** -0.5 * LOG2E, tm=tm_in, tn=INPROJ_TN)

    q = q.reshape(b, t, wa)
    k16 = k16.reshape(b, t, wa)
    v16 = v16.reshape(b, t, wa)
    if past is None:
        bound = (8.0 * 1.01 * LOG2E) * jnp.max(jnp.abs(wts["q_norm_g"])) * jnp.max(jnp.abs(wts["k_norm_g"]))
        attn = _attn_prompt(q, k16, v16, lams, wts["subln_g"], lam_init, bound,
                            tq=_pick(t, 1024), tk=_pick(t, 2048))
        tb, ng = _pick(t, 256), 4
    else:
        attn = _attn_sample(q, *past, k16, v16, lams, wts["subln_g"], lam_init)
        tb, ng = _pick(t, CHUNK_B), wb // LANES

    p_rwkv = p_rwkv.reshape(b, t, -1)
    rw, s_bd = _rwkv(p_rwkv, shift0, s0, wts["mu"], wts["vecs"], wts["loras"], tb=tb, ng=ng)

    y = _merge_ffn(x2d, attn.reshape(n, wa), rw.reshape(n, wb), wts["w_out"], wts["norm2_g"],
                   wts["w_gate"], wts["w_up"], wts["w_down"], tm=tm, tf=512)

    nh_a = wa // HEAD_A
    rcols = wts["rwkv_cols"]
    return (y.reshape(b, t, d), k32.reshape(b, t, nh_a, HEAD_A), v32.reshape(b, t, nh_a, HEAD_A),
            _unblock_state(s_bd), p_rwkv[:, t - 1:, :rcols])


def kernel(x_prompt, x_sample, cache_attn_k, cache_attn_v, state_rwkv, state_rwkv_shift, norm1_g, w_in, q_norm_g, k_norm_g, lambda_q1, lambda_k1, lambda_q2, lambda_k2, subln_g, mu_rwkv, w0, w2, a0, a2, g2, k_k, k_a, r_k, lnx_g, lnx_b, w_out, norm2_g, w_gate, w_up, w_down):
    depth = w_in.shape[0]
    bp, tp, d = x_prompt.shape
    bs, ts, _ = x_sample.shape
    past_len = cache_attn_k.shape[2]
    nh_a = cache_attn_k.shape[3]
    wa = nh_a * HEAD_A
    wb = d - wa
    rcols = 3 * wb + 3 * LORA
    pad = -rcols % INPROJ_TN
    pos_p = jnp.arange(tp, dtype=jnp.int32)
    pos_s = past_len + jnp.arange(ts, dtype=jnp.int32)

    xp, xs = x_prompt, x_sample
    outs = [[] for _ in range(8)]
    for l in range(depth):
        lam_init = 0.8 - 0.6 * math.exp(-0.3 * l)

        def lora_pad(w, slot):
            return jnp.pad(w, ((slot * LORA, 2 * LANES - (slot + 1) * LORA), (0, 0)))

        row = lambda v: v.reshape(1, -1)
        wts = dict(
            width_a=wa, width_b=wb, rwkv_cols=rcols,
            norm1_g=norm1_g[l], norm2_g=norm2_g[l],
            w_in=jnp.pad(w_in[l], ((0, 0), (0, pad))).astype(BF16),
            q_norm_g=row(q_norm_g[l]), k_norm_g=row(k_norm_g[l]),
            lams=[row(lambda_q1[l]), row(lambda_k1[l]), row(lambda_q2[l]), row(lambda_k2[l])],
            subln_g=row(subln_g[l]),
            mu=jnp.pad(row(mu_rwkv[l]), ((0, 0), (0, pad))),
            vecs=[row(w0[l]), row(a0[l]), row(k_k[l]), row(k_a[l]), row(r_k[l]), row(lnx_g[l]), row(lnx_b[l])],
            loras=[lora_pad(w2[l], 0).astype(BF16), lora_pad(a2[l], 1).astype(BF16),
                   lora_pad(g2[l], 2).astype(BF16)],
            w_out=w_out[l].astype(BF16), w_gate=w_gate[l].astype(BF16),
            w_up=w_up[l].astype(BF16), w_down=w_down[l].astype(BF16),
        )
        s0_p = jnp.zeros((bp, wb // LANES, LANES, LANES), F32)
        shift0_p = jnp.zeros((bp, 1, rcols + pad), F32)
        xp, kp, vp, sp, shp = _layer(xp, pos_p, None, s0_p, shift0_p, lam_init, wts)
        past = (cache_attn_k, cache_attn_v, l)
        shift0_s = jnp.pad(state_rwkv_shift[l], ((0, 0), (0, 0), (0, pad)))
        xs, ks, vs, ss, shs = _layer(xs, pos_s, past, _block_diag_state(state_rwkv[l]), shift0_s,
                                     lam_init, wts)
        for lst, val in zip(outs, (kp, vp, sp, shp, ks, vs, ss, shs)):
            lst.append(val)
    return (xp, xs) + tuple(jnp.stack(o) for o in outs)
```
